```python
import math
import jax, jax.numpy as jnp
from jax import lax
import numpy as np

D_MODEL = 1024
BATCH = 4
SEQ = 8192
DEPTH = 2

N_META = 16
MLA_HEADS = 8
Q_LORA = 768
KV_LORA = 256
QK_NOPE = 128
QK_ROPE = 64
V_HEAD = 128
ROPE_THETA = 10000.0
Q_BLOCK = 128
NEG_INF = -1e30
SSD_INNER = 2 * D_MODEL
SSD_HEAD_DIM = 64
SSD_HEADS = SSD_INNER // SSD_HEAD_DIM
SSD_GROUPS = 4
SSD_HEADS_PER_GROUP = SSD_HEADS // SSD_GROUPS
SSD_STATE = 128
SSD_CONV = 4
SSD_CONV_DIM = SSD_INNER + 2 * SSD_GROUPS * SSD_STATE
CHUNK = 128
DT_MIN = 0.001
DT_MAX = 0.1
D_FF = 2816
FFN_CONV = 3
LN_EPS = 1e-5
RMS_EPS = 1e-6
DEEPNORM_ALPHA = (2 * DEPTH) ** 0.25
DEEPNORM_BETA = (8 * DEPTH) ** -0.25
IN_SIZES = (Q_LORA, KV_LORA, QK_ROPE, SSD_INNER, SSD_CONV_DIM, SSD_HEADS, D_MODEL, D_MODEL)
IN_COLS = sum(IN_SIZES)

kernel_name = "hybrid_mla_ssd_gated_deepnorm"


def layer_norm(x, g, b):
    xf = x.astype(jnp.float32)
    mu = jnp.mean(xf, axis=-1, keepdims=True)
    var = jnp.mean(jnp.square(xf - mu), axis=-1, keepdims=True)
    y = (xf - mu) * lax.rsqrt(var + LN_EPS) * g.astype(jnp.float32) + b.astype(jnp.float32)
    return y.astype(x.dtype)


def rms_norm(x, g):
    xf = x.astype(jnp.float32)
    y = xf * lax.rsqrt(jnp.mean(xf * xf, axis=-1, keepdims=True) + RMS_EPS) * g.astype(jnp.float32)
    return y.astype(x.dtype)


def causal_dwconv(x, w, b):
    k = w.shape[0]
    y = lax.conv_general_dilated(
        x, w[:, None, :].astype(x.dtype), window_strides=(1,), padding=((k - 1, 0),),
        dimension_numbers=("NWC", "WIO", "NWC"), feature_group_count=x.shape[-1])
    return y + b.astype(x.dtype)


def front_pad(t, pad):
    return jnp.pad(t, [(0, 0), (pad, 0)] + [(0, 0)] * (t.ndim - 2))


def rope_tables(length):
    inv_freq = 1.0 / (ROPE_THETA ** (jnp.arange(0, QK_ROPE, 2, dtype=jnp.float32) / QK_ROPE))
    ang = jnp.arange(length, dtype=jnp.float32)[:, None] * inv_freq[None, :]
    ang = jnp.concatenate([ang, ang], axis=-1)
    return jnp.cos(ang), jnp.sin(ang)


def apply_rope(x, cos, sin):
    xf = x.astype(jnp.float32)
    x1, x2 = jnp.split(xf, 2, axis=-1)
    rot = jnp.concatenate([-x2, x1], axis=-1)
    return (xf * cos + rot * sin).astype(x.dtype)


def mla_branch(q_lat, kv_lat, k_pe, cos, sin, q_norm_g, w_q_b, kv_norm_g, w_kv_b, w_o):
    b, l, _ = q_lat.shape
    q = (rms_norm(q_lat, q_norm_g) @ w_q_b).reshape(b, l, MLA_HEADS, QK_NOPE + QK_ROPE)
    q_nope = q[..., :QK_NOPE]
    q_pe = apply_rope(q[..., QK_NOPE:], cos[:, None, :], sin[:, None, :])
    kv = (rms_norm(kv_lat, kv_norm_g) @ w_kv_b).reshape(b, l, MLA_HEADS, QK_NOPE + V_HEAD)
    k_nope, v = kv[..., :QK_NOPE], kv[..., QK_NOPE:]
    k_pe = apply_rope(k_pe, cos, sin)
    pad = (-l) % Q_BLOCK
    q_nope, q_pe, k_nope, k_pe, v = [front_pad(t, pad) for t in (q_nope, q_pe, k_nope, k_pe, v)]
    lp = l + pad
    scale = (QK_NOPE + QK_ROPE) ** -0.5
    key_pos = jnp.arange(lp)

    def attend_block(blk):
        start = blk * Q_BLOCK
        qn = lax.dynamic_slice_in_dim(q_nope, start, Q_BLOCK, axis=1)
        qr = lax.dynamic_slice_in_dim(q_pe, start, Q_BLOCK, axis=1)
        s = (jnp.einsum("bqhd,bkhd->bhqk", qn, k_nope).astype(jnp.float32)
             + jnp.einsum("bqhr,bkr->bhqk", qr, k_pe).astype(jnp.float32))
        q_pos = start + jnp.arange(Q_BLOCK)
        visible = (key_pos[None, :] <= q_pos[:, None]) & (key_pos[None, :] >= pad)
        p = jax.nn.softmax(jnp.where(visible, s * scale, NEG_INF), axis=-1)
        return jnp.einsum("bhqk,bkhd->bqhd", p.astype(v.dtype), v)

    o = lax.map(attend_block, jnp.arange(lp // Q_BLOCK))
    o = jnp.moveaxis(o, 0, 1).reshape(b, lp, MLA_HEADS * V_HEAD)[:, pad:]
    return o @ w_o


def ssd_branch(z, xbc, dt_raw, conv_w, conv_b, dt_bias, a_log, d_skip, norm_g, w_o):
    b, l, _ = xbc.shape
    dtype = xbc.dtype
    f32 = jnp.float32
    xbc = jax.nn.silu(causal_dwconv(xbc, conv_w, conv_b)).astype(f32)
    xs, bm, cm = jnp.split(xbc, [SSD_INNER, SSD_INNER + SSD_GROUPS * SSD_STATE], axis=-1)
    dt = jax.nn.softplus(dt_raw.astype(f32) + dt_bias.astype(f32))
    a = -jnp.exp(a_log.astype(f32))
    pad = (-l) % CHUNK
    lp = l + pad
    nc = lp // CHUNK
    g, e = SSD_GROUPS, SSD_HEADS_PER_GROUP
    dt_c = front_pad(dt, pad).reshape(b, nc, CHUNK, g, e)
    x_c = front_pad(xs, pad).reshape(b, nc, CHUNK, g, e, SSD_HEAD_DIM) * dt_c[..., None]
    b_c = front_pad(bm, pad).reshape(b, nc, CHUNK, g, SSD_STATE)
    c_c = front_pad(cm, pad).reshape(b, nc, CHUNK, g, SSD_STATE)
    a_c = jnp.transpose(dt_c * a.reshape(g, e), (0, 3, 4, 1, 2))
    a_cs = jnp.cumsum(a_c, axis=-1)
    causal = jnp.tril(jnp.ones((CHUNK, CHUNK), dtype=bool))
    decay = jnp.exp(jnp.where(causal, a_cs[..., :, None] - a_cs[..., None, :], -jnp.inf))
    cb = jnp.einsum("bclgn,bcsgn->bgcls", c_c, b_c)
    y_diag = jnp.einsum("bgecls,bcsgep->bclgep", cb[:, :, None] * decay, x_c)
    decay_states = jnp.exp(a_cs[..., -1:] - a_cs)
    states = jnp.einsum("bclgn,bgecl,bclgep->cbgepn", b_c, decay_states, x_c)
    chunk_decay = jnp.moveaxis(jnp.exp(a_cs[..., -1]), -1, 0)

    def carry_state(h, inp):
        s_c, d_c = inp
        return d_c[..., None, None] * h + s_c, h

    h0 = jnp.zeros(states.shape[1:], f32)
    _, prev = lax.scan(carry_state, h0, (states, chunk_decay))
    y_off = jnp.einsum("bclgn,cbgepn,bgecl->bclgep", c_c, prev, jnp.exp(a_cs))
    y = (y_diag + y_off).reshape(b, lp, SSD_HEADS, SSD_HEAD_DIM)[:, pad:]
    y = y + xs.reshape(b, l, SSD_HEADS, SSD_HEAD_DIM) * d_skip.astype(f32)[:, None]
    gs = SSD_INNER // SSD_GROUPS
    y = y.reshape(b, l, SSD_GROUPS, gs) * jax.nn.silu(z.astype(f32)).reshape(b, l, SSD_GROUPS, gs)
    y = y * lax.rsqrt(jnp.mean(y * y, axis=-1, keepdims=True) + RMS_EPS)
    y = y.reshape(b, l, SSD_INNER) * norm_g.astype(f32)
    return y.astype(dtype) @ w_o


def conv_glu_ffn(h, w_up, conv_w, conv_b, w_down):
    u = causal_dwconv(h @ w_up, conv_w, conv_b)
    gate, val = jnp.split(u, 2, axis=-1)
    return (jax.nn.silu(gate) * val) @ w_down


def setup_inputs(seed: int = 0) -> dict:
    key = jax.random.key(seed)
    ks = iter(jax.random.split(key, 40))
    nrm = lambda shape, scale: jax.random.normal(next(ks), shape, jnp.float32) * scale
    gain = lambda shape: 1.0 + nrm(shape, 0.02)
    bias = lambda shape: nrm(shape, 0.02)
    L = DEPTH
    x = nrm((BATCH, SEQ, D_MODEL), 1.0)
    meta_tokens = nrm((N_META, D_MODEL), 1.0)
    emb_ln_g = gain((D_MODEL,))
    emb_ln_b = bias((D_MODEL,))
    w_in = nrm((L, D_MODEL, IN_COLS), D_MODEL ** -0.5)
    q_norm_g = gain((L, Q_LORA))
    w_q_b = nrm((L, Q_LORA, MLA_HEADS * (QK_NOPE + QK_ROPE)), Q_LORA ** -0.5)
    kv_norm_g = gain((L, KV_LORA))
    w_kv_b = nrm((L, KV_LORA, MLA_HEADS * (QK_NOPE + V_HEAD)), KV_LORA ** -0.5)
    w_o_attn = nrm((L, MLA_HEADS * V_HEAD, D_MODEL), (MLA_HEADS * V_HEAD) ** -0.5)
    ssd_conv_w = nrm((L, SSD_CONV, SSD_CONV_DIM), SSD_CONV ** -0.5)
    ssd_conv_b = bias((L, SSD_CONV_DIM))
    u = jax.random.uniform(next(ks), (L, SSD_HEADS), jnp.float32)
    dt0 = jnp.exp(u * (math.log(DT_MAX) - math.log(DT_MIN)) + math.log(DT_MIN))
    dt_bias = dt0 + jnp.log(-jnp.expm1(-dt0))
    a_log = jnp.log(jax.random.uniform(next(ks), (L, SSD_HEADS), jnp.float32, 1.0, 16.0))
    d_skip = gain((L, SSD_HEADS))
    ssd_norm_g = gain((L, SSD_INNER))
    w_o_ssd = nrm((L, SSD_INNER, D_MODEL), SSD_INNER ** -0.5)
    w_out = nrm((L, D_MODEL, D_MODEL), DEEPNORM_BETA * D_MODEL ** -0.5)
    ln1_g = gain((L, D_MODEL))
    ln1_b = bias((L, D_MODEL))
    w_up = nrm((L, D_MODEL, 2 * D_FF), D_MODEL ** -0.5)
    ffn_conv_w = nrm((L, FFN_CONV, 2 * D_FF), FFN_CONV ** -0.5)
    ffn_conv_b = bias((L, 2 * D_FF))
    w_down = nrm((L, D_FF, D_MODEL), DEEPNORM_BETA * D_FF ** -0.5)
    ln2_g = gain((L, D_MODEL))
    ln2_b = bias((L, D_MODEL))
    return {"x": x, "meta_tokens": meta_tokens, "emb_ln_g": emb_ln_g, "emb_ln_b": emb_ln_b,
            "w_in": w_in, "q_norm_g": q_norm_g, "w_q_b": w_q_b, "kv_norm_g": kv_norm_g,
            "w_kv_b": w_kv_b, "w_o_attn": w_o_attn, "ssd_conv_w": ssd_conv_w, "ssd_conv_b": ssd_conv_b,
            "dt_bias": dt_bias, "a_log": a_log, "d_skip": d_skip, "ssd_norm_g": ssd_norm_g,
            "w_o_ssd": w_o_ssd, "w_out": w_out, "ln1_g": ln1_g, "ln1_b": ln1_b, "w_up": w_up,
            "ffn_conv_w": ffn_conv_w, "ffn_conv_b": ffn_conv_b, "w_down": w_down,
            "ln2_g": ln2_g, "ln2_b": ln2_b}


def reference(x, meta_tokens, emb_ln_g, emb_ln_b, w_in, q_norm_g, w_q_b, kv_norm_g, w_kv_b,
              w_o_attn, ssd_conv_w, ssd_conv_b, dt_bias, a_log, d_skip, ssd_norm_g, w_o_ssd,
              w_out, ln1_g, ln1_b, w_up, ffn_conv_w, ffn_conv_b, w_down, ln2_g, ln2_b):
    b = x.shape[0]
    meta = jnp.broadcast_to(meta_tokens.astype(x.dtype)[None], (b, N_META, D_MODEL))
    h = layer_norm(jnp.concatenate([meta, x], axis=1), emb_ln_g, emb_ln_b)
    cos, sin = rope_tables(h.shape[1])
    splits = np.cumsum(IN_SIZES)[:-1].tolist()
    for i in range(DEPTH):
        proj = h @ w_in[i]
        q_lat, kv_lat, k_pe, z, xbc, dt_raw, g_attn, g_ssd = jnp.split(proj, splits, axis=-1)
        y_attn = mla_branch(q_lat, kv_lat, k_pe, cos, sin, q_norm_g[i], w_q_b[i],
                            kv_norm_g[i], w_kv_b[i], w_o_attn[i])
        y_ssd = ssd_branch(z, xbc, dt_raw, ssd_conv_w[i], ssd_conv_b[i], dt_bias[i], a_log[i],
                           d_skip[i], ssd_norm_g[i], w_o_ssd[i])
        mixed = jax.nn.sigmoid(g_attn) * y_attn + jax.nn.sigmoid(g_ssd) * y_ssd
        h = layer_norm(DEEPNORM_ALPHA * h + mixed @ w_out[i], ln1_g[i], ln1_b[i])
        ffn = conv_glu_ffn(h, w_up[i], ffn_conv_w[i], ffn_conv_b[i], w_down[i])
        h = layer_norm(DEEPNORM_ALPHA * h + ffn, ln2_g[i], ln2_b[i])
    return h[:, N_META:]
```

```python
import functools
import math

import jax
import jax.numpy as jnp
from jax import lax
from jax.experimental import pallas as pl
from jax.experimental.pallas import tpu as pltpu

F32 = jnp.float32
BF16 = jnp.bfloat16

N_META = 16
MLA_HEADS = 8
Q_LORA = 768
KV_LORA = 256
QK_NOPE = 128
QK_ROPE = 64
V_HEAD = 128
ROPE_THETA = 10000.0
SSD_HEAD_DIM = 64
SSD_GROUPS = 4
SSD_STATE = 128
SSD_CONV = 4
FFN_CONV = 3
LN_EPS = 1e-5
RMS_EPS = 1e-6
NEG_BIG = -1e30

LANES = 128
SUBLANES = 8
ROW_ALIGN = 128
HEAD_SLOT = 256
VMEM_LIMIT = 56 * 1024 * 1024

ROW_BLOCK = 640
FF_CHUNK = 256


def _const_spec(shape):
    nd = len(shape)
    return pl.BlockSpec(shape, lambda *_: (0,) * nd, pipeline_mode=pl.Buffered(1))


def _params(*sem):
    return pltpu.CompilerParams(dimension_semantics=sem, vmem_limit_bytes=VMEM_LIMIT)


def _layer_norm(x, g, b):
    mu = jnp.mean(x, axis=-1, keepdims=True)
    xc = x - mu
    var = jnp.mean(xc * xc, axis=-1, keepdims=True)
    return xc * lax.rsqrt(var + LN_EPS) * g + b


def _rms_norm(x, g):
    return x * lax.rsqrt(jnp.mean(x * x, axis=-1, keepdims=True) + RMS_EPS) * g


def _silu(x):
    return x * jax.nn.sigmoid(x)


def _softplus(x):
    return jnp.maximum(x, 0.0) + jnp.log1p(jnp.exp(-jnp.abs(x)))


def _mm(a, b):
    return jnp.dot(a, b, preferred_element_type=F32)


def _embed_kernel(x_ref, meta_ref, g_ref, b_ref, o_ref, *, pad):
    j = pl.program_id(1)

    @pl.when(j == 0)
    def _():
        o_ref[0, :pad, :] = jnp.zeros((pad, o_ref.shape[2]), F32)
        o_ref[0, pad:, :] = _layer_norm(meta_ref[...], g_ref[...], b_ref[...])

    @pl.when(j > 0)
    def _():
        o_ref[0] = _layer_norm(x_ref[0], g_ref[...], b_ref[...])


def _embed(x, meta, g, b, pad, lp):
    bsz, _, d = x.shape
    nblk = lp // ROW_ALIGN
    return pl.pallas_call(
        functools.partial(_embed_kernel, pad=pad),
        grid=(bsz, nblk),
        in_specs=[
            pl.BlockSpec((1, ROW_ALIGN, d), lambda bi, j: (bi, jnp.maximum(j - 1, 0), 0)),
            _const_spec(meta.shape),
            _const_spec(g.shape),
            _const_spec(b.shape),
        ],
        out_specs=pl.BlockSpec((1, ROW_ALIGN, d), lambda bi, j: (bi, j, 0)),
        out_shape=jax.ShapeDtypeStruct((bsz, lp, d), F32),
        compiler_params=_params("parallel", "arbitrary"),
        name="embed_ln",
    )(x, meta, g, b)


def _attn_front_kernel(h_ref, wa_ref, qg_ref, wq_ref, kvg_ref, wkv_ref, cos_ref, sin_ref,
                       q_ref, k_ref, v_ref, *, pad, nb, qscale):
    tm = h_ref.shape[0]
    i = pl.program_id(0)
    hb = h_ref[...].astype(BF16)
    pa = _mm(hb, wa_ref[...])
    q_lat = pa[:, :Q_LORA]
    kv_lat = pa[:, Q_LORA:Q_LORA + KV_LORA]
    kp = pa[:, Q_LORA + KV_LORA:]
    cos_t = cos_ref[...]
    sin_t = sin_ref[...]
    lane = lax.broadcasted_iota(jnp.int32, (tm, LANES), 1)
    flag_lane = lane == QK_ROPE

    qn = _rms_norm(q_lat, qg_ref[...]).astype(BF16)
    qf = _mm(qn, wq_ref[...])
    for hh in range(MLA_HEADS):
        base = hh * HEAD_SLOT
        q_ref[:, base:base + QK_NOPE] = (qf[:, base:base + QK_NOPE] * qscale).astype(BF16)
        t = qf[:, base + QK_NOPE:base + HEAD_SLOT]
        r = (t * cos_t + pltpu.roll(t, QK_ROPE, 1) * sin_t) * qscale
        q_ref[:, base + QK_NOPE:base + HEAD_SLOT] = jnp.where(flag_lane, 1.0, r).astype(BF16)

    kvn = _rms_norm(kv_lat, kvg_ref[...]).astype(BF16)
    kvf = _mm(kvn, wkv_ref[...])
    rowid = lax.broadcasted_iota(jnp.int32, (tm, LANES), 0) + (i % nb) * tm
    kt = kp * cos_t + pltpu.roll(kp, QK_ROPE, 1) * sin_t
    kt = jnp.where(flag_lane, jnp.where(rowid < pad, NEG_BIG, 0.0), kt).astype(BF16)
    nk = MLA_HEADS * QK_NOPE
    for hh in range(MLA_HEADS):
        base = hh * HEAD_SLOT
        k_ref[:, base:base + QK_NOPE] = kvf[:, hh * QK_NOPE:(hh + 1) * QK_NOPE].astype(BF16)
        k_ref[:, base + QK_NOPE:base + HEAD_SLOT] = kt
    v_ref[...] = kvf[:, nk:].astype(BF16)


def _attn_front(h, wa, qg, wq, kvg, wkv, cos_t, sin_t, pad, lp, qscale):
    t, d = h.shape
    tm = ROW_BLOCK
    nb = lp // tm
    qk_cols = MLA_HEADS * HEAD_SLOT
    v_cols = MLA_HEADS * V_HEAD
    row = lambda i: (i, 0)
    return pl.pallas_call(
        functools.partial(_attn_front_kernel, pad=pad, nb=nb, qscale=qscale),
        grid=(t // tm,),
        in_specs=[
            pl.BlockSpec((tm, d), row),
            _const_spec(wa.shape), _const_spec(qg.shape), _const_spec(wq.shape),
            _const_spec(kvg.shape), _const_spec(wkv.shape),
            pl.BlockSpec((tm, LANES), lambda i: (i % nb, 0)),
            pl.BlockSpec((tm, LANES), lambda i: (i % nb, 0)),
        ],
        out_specs=[pl.BlockSpec((tm, qk_cols), row), pl.BlockSpec((tm, qk_cols), row),
                   pl.BlockSpec((tm, v_cols), row)],
        out_shape=[jax.ShapeDtypeStruct((t, qk_cols), BF16), jax.ShapeDtypeStruct((t, qk_cols), BF16),
                   jax.ShapeDtypeStruct((t, v_cols), BF16)],
        compiler_params=_params("parallel"),
        name="attn_front",
    )(h, wa, qg, wq, kvg, wkv, cos_t, sin_t)


def _attn_kernel(q_ref, k_ref, v_ref, o_ref, m_scr, l_scr, acc_scr):
    bq = q_ref.shape[0]
    i = pl.program_id(2)
    q = q_ref[...]

    def scores(j):
        kj = k_ref[pl.ds(pl.multiple_of(j * bq, bq), bq), :]
        return lax.dot_general(q, kj, (((1,), (1,)), ((), ())), preferred_element_type=F32)

    def values(j):
        return v_ref[pl.ds(pl.multiple_of(j * bq, bq), bq), :]

    row = lax.broadcasted_iota(jnp.int32, (bq, bq), 0)
    col = lax.broadcasted_iota(jnp.int32, (bq, bq), 1)
    s = jnp.where(col <= row, scores(i), NEG_BIG)
    m = jnp.max(s, axis=-1, keepdims=True)
    p = jnp.exp2(s - m)
    m_scr[...] = m
    l_scr[...] = jnp.sum(p, axis=-1, keepdims=True)
    acc_scr[...] = _mm(p.astype(BF16), values(i))

    def body(j, carry):
        s = scores(j)
        m_prev = m_scr[...]
        m_new = jnp.maximum(m_prev, jnp.max(s, axis=-1, keepdims=True))
        alpha = jnp.exp2(m_prev - m_new)
        p = jnp.exp2(s - m_new)
        l_scr[...] = alpha * l_scr[...] + jnp.sum(p, axis=-1, keepdims=True)
        acc_scr[...] = alpha * acc_scr[...] + _mm(p.astype(BF16), values(j))
        m_scr[...] = m_new
        return carry

    lax.fori_loop(0, i, body, 0)
    o_ref[...] = (acc_scr[...] / l_scr[...]).astype(o_ref.dtype)


def _attention(q, k, v, bsz, lp):
    bq = ROW_BLOCK
    nq = lp // bq
    t = q.shape[0]
    return pl.pallas_call(
        _attn_kernel,
        grid=(bsz, MLA_HEADS, nq),
        in_specs=[
            pl.BlockSpec((bq, HEAD_SLOT), lambda b, h, i: (b * nq + i, h)),
            pl.BlockSpec((lp, HEAD_SLOT), lambda b, h, i: (b, h)),
            pl.BlockSpec((lp, V_HEAD), lambda b, h, i: (b, h)),
        ],
        out_specs=pl.BlockSpec((bq, V_HEAD), lambda b, h, i: (b * nq + i, h)),
        out_shape=jax.ShapeDtypeStruct((t, MLA_HEADS * V_HEAD), BF16),
        scratch_shapes=[pltpu.VMEM((bq, 1), F32), pltpu.VMEM((bq, 1), F32), pltpu.VMEM((bq, V_HEAD), F32)],
        compiler_params=_params("parallel", "parallel", "arbitrary"),
        name="flash_attn",
    )(q, k, v)


def _proj_kernel(h_ref, w_ref, dt_ref, z_ref, xbc_ref, ga_ref, gs_ref):
    hb = h_ref[...].astype(BF16)
    c0 = 0
    for ref in (dt_ref, z_ref, xbc_ref, ga_ref, gs_ref):
        n = ref.shape[1]
        ref[...] = _mm(hb, w_ref[:, c0:c0 + n])
        c0 += n


def _proj_rest(h, wb, widths):
    t, d = h.shape
    tm = ROW_BLOCK // 2
    row = lambda i: (i, 0)
    return pl.pallas_call(
        _proj_kernel,
        grid=(t // tm,),
        in_specs=[pl.BlockSpec((tm, d), row), _const_spec(wb.shape)],
        out_specs=[pl.BlockSpec((tm, n), row) for n in widths],
        out_shape=[jax.ShapeDtypeStruct((t, n), F32) for n in widths],
        compiler_params=_params("parallel"),
        name="proj_rest",
    )(h, wb)


def _ssd_kernel(xbc_ref, halo_ref, dt_ref, z_ref, cw_ref, cb_ref, dtb_ref, alog_ref, dskip_ref,
                ng_ref, expand_ref, y_ref, ext_scr, state_scr, *, pad, inner):
    q = xbc_ref.shape[0]
    c = pl.program_id(1)
    gw = inner // SSD_GROUPS
    hpg = gw // SSD_HEAD_DIM
    n = SSD_STATE

    @pl.when(c == 0)
    def _():
        state_scr[...] = jnp.zeros(state_scr.shape, F32)

    ext_scr[0:SUBLANES, :] = halo_ref[...]
    ext_scr[SUBLANES:, :] = xbc_ref[...]
    conv = cb_ref[...]
    for kk in range(SSD_CONV):
        conv = conv + cw_ref[kk:kk + 1, :] * ext_scr[pl.ds(SUBLANES - SSD_CONV + 1 + kk, q), :]
    rowid = lax.broadcasted_iota(jnp.int32, (q, 1), 0)
    valid = jnp.logical_or(c > 0, rowid >= pad)
    u = jnp.where(valid, _silu(conv), 0.0)
    xs = u[:, :inner]
    bm = u[:, inner:inner + SSD_GROUPS * n]
    cm = u[:, inner + SSD_GROUPS * n:]

    dt = jnp.where(valid, _softplus(dt_ref[...] + dtb_ref[...]), 0.0)
    a = -jnp.exp(alog_ref[...])
    d_a = dt * a
    r_i = lax.broadcasted_iota(jnp.int32, (q, q), 0)
    c_i = lax.broadcasted_iota(jnp.int32, (q, q), 1)
    causal = c_i <= r_i
    tril = jnp.where(causal, 1.0, 0.0).astype(F32)
    a_cs = jnp.dot(tril, d_a, preferred_element_type=F32, precision=lax.Precision.HIGHEST)
    a_last = a_cs[q - 1:q, :]
    a_cs_t = a_cs.T

    small = jnp.concatenate([dt, jnp.exp(a_cs), jnp.exp(a_last - a_cs)], axis=0)
    hi = small.astype(BF16)
    lo = (small - hi.astype(F32)).astype(BF16)
    wide = _mm(jnp.concatenate([hi, lo], axis=1), expand_ref[...])
    dt_x = wide[0:q]
    ea_x = wide[q:2 * q]
    ds_x = wide[2 * q:3 * q]

    xc = xs * dt_x
    xcb = xc.astype(BF16)
    xd = (xc * ds_x).astype(BF16)
    lane = lax.broadcasted_iota(jnp.int32, (q, LANES), 1)
    low_half = lane < SSD_HEAD_DIM

    for g in range(SSD_GROUPS):
        cg = cm[:, g * n:(g + 1) * n].astype(BF16)
        bg = bm[:, g * n:(g + 1) * n].astype(BF16)
        cbg = lax.dot_general(cg, bg, (((1,), (1,)), ((), ())), preferred_element_type=F32)
        h_prev = state_scr[g]
        y_off = _mm(cg, h_prev.astype(BF16))
        parts = []
        for pr in range(hpg // 2):
            e0 = g * hpg + 2 * pr
            ms = []
            for e in (e0, e0 + 1):
                diff = a_cs[:, e:e + 1] - a_cs_t[e:e + 1, :]
                ms.append((cbg * jnp.exp(jnp.where(causal, diff, NEG_BIG))).astype(BF16))
            x2 = xcb[:, e0 * SSD_HEAD_DIM:(e0 + 2) * SSD_HEAD_DIM]
            zero = jnp.zeros_like(x2)
            rhs = jnp.concatenate([jnp.where(low_half, x2, zero), jnp.where(low_half, zero, x2)], axis=0)
            parts.append(_mm(jnp.concatenate(ms, axis=1), rhs))
        sl = slice(g * gw, (g + 1) * gw)
        y = jnp.concatenate(parts, axis=1) + y_off * ea_x[:, sl]
        s_new = lax.dot_general(bg, xd[:, sl], (((0,), (0,)), ((), ())), preferred_element_type=F32)
        state_scr[g] = ea_x[q - 1:q, sl] * h_prev + s_new
        y = y + xs[:, sl] * dskip_ref[:, sl]
        y = y * _silu(z_ref[:, sl])
        y = y * lax.rsqrt(jnp.mean(y * y, axis=-1, keepdims=True) + RMS_EPS) * ng_ref[:, sl]
        y_ref[:, sl] = y.astype(y_ref.dtype)


def _ssd(xbc, dt_raw, z, conv_w, conv_b, dt_bias, a_log, dskip_x, norm_g, expand, bsz, lp, pad):
    t, cdim = xbc.shape
    inner = z.shape[1]
    q = ROW_ALIGN
    nc = lp // q
    row = lambda b, c: (b * nc + c, 0)
    halo = lambda b, c: (jnp.maximum((b * nc + c) * (q // SUBLANES) - 1, 0), 0)
    return pl.pallas_call(
        functools.partial(_ssd_kernel, pad=pad, inner=inner),
        grid=(bsz, nc),
        in_specs=[
            pl.BlockSpec((q, cdim), row),
            pl.BlockSpec((SUBLANES, cdim), halo),
            pl.BlockSpec((q, LANES), row),
            pl.BlockSpec((q, inner), row),
            _const_spec(conv_w.shape), _const_spec(conv_b.shape), _const_spec(dt_bias.shape),
            _const_spec(a_log.shape), _const_spec(dskip_x.shape), _const_spec(norm_g.shape),
            _const_spec(expand.shape),
        ],
        out_specs=pl.BlockSpec((q, inner), row),
        out_shape=jax.ShapeDtypeStruct((t, inner), BF16),
        scratch_shapes=[pltpu.VMEM((q + SUBLANES, cdim), F32),
                        pltpu.VMEM((SSD_GROUPS, SSD_STATE, inner // SSD_GROUPS), F32)],
        compiler_params=_params("parallel", "arbitrary"),
        name="ssd_mixer",
    )(xbc, xbc, dt_raw, z, conv_w, conv_b, dt_bias, a_log, dskip_x, norm_g, expand)


def _mix_kernel(o_ref, y_ref, ga_ref, gs_ref, h_ref, woa_ref, wos_ref, wout_ref, g_ref, b_ref, out_ref,
                *, alpha, pad, nb):
    tm = h_ref.shape[0]
    i = pl.program_id(0)
    ya = _mm(o_ref[...], woa_ref[...])
    ys = _mm(y_ref[...], wos_ref[...])
    mixed = jax.nn.sigmoid(ga_ref[...]) * ya + jax.nn.sigmoid(gs_ref[...]) * ys
    r = alpha * h_ref[...] + _mm(mixed.astype(BF16), wout_ref[...])
    rowid = lax.broadcasted_iota(jnp.int32, (tm, 1), 0) + (i % nb) * tm
    out_ref[...] = jnp.where(rowid >= pad, _layer_norm(r, g_ref[...], b_ref[...]), 0.0)


def _mix(o, y, ga, gs, h, woa, wos, wout, g, b, alpha, pad, lp):
    t, d = h.shape
    tm = ROW_BLOCK
    nb = lp // tm
    row = lambda i: (i, 0)
    return pl.pallas_call(
        functools.partial(_mix_kernel, alpha=alpha, pad=pad, nb=nb),
        grid=(t // tm,),
        in_specs=[
            pl.BlockSpec((tm, o.shape[1]), row), pl.BlockSpec((tm, y.shape[1]), row),
            pl.BlockSpec((tm, d), row), pl.BlockSpec((tm, d), row), pl.BlockSpec((tm, d), row),
            _const_spec(woa.shape), _const_spec(wos.shape), _const_spec(wout.shape),
            _const_spec(g.shape), _const_spec(b.shape),
        ],
        out_specs=pl.BlockSpec((tm, d), row),
        out_shape=jax.ShapeDtypeStruct((t, d), F32),
        compiler_params=_params("parallel"),
        name="mix_ln",
    )(o, y, ga, gs, h, woa, wos, wout, g, b)


def _ffn_kernel(h_ref, halo_ref, wup_ref, cw_ref, cb_ref, wdown_ref, g_ref, b_ref, out_ref,
                u_scr, acc_scr, *, alpha, pad, nb):
    tm = h_ref.shape[0]
    i = pl.program_id(0)
    nchunk = wup_ref.shape[0]
    ck = wdown_ref.shape[1]
    hx = jnp.concatenate([halo_ref[...], h_ref[...]], axis=0).astype(BF16)
    acc_scr[...] = jnp.zeros(acc_scr.shape, F32)

    def body(c, carry):
        u_scr[...] = _mm(hx, wup_ref[c])
        cw = cw_ref[c]
        v = cb_ref[c]
        for kk in range(FFN_CONV):
            v = v + cw[kk:kk + 1, :] * u_scr[pl.ds(SUBLANES - FFN_CONV + 1 + kk, tm), :]
        act = (_silu(v[:, :ck]) * v[:, ck:]).astype(BF16)
        acc_scr[...] += _mm(act, wdown_ref[c])
        return carry

    lax.fori_loop(0, nchunk, body, 0)
    r = alpha * h_ref[...] + acc_scr[...]
    rowid = lax.broadcasted_iota(jnp.int32, (tm, 1), 0) + (i % nb) * tm
    out_ref[...] = jnp.where(rowid >= pad, _layer_norm(r, g_ref[...], b_ref[...]), 0.0)


def _ffn(h, wup, cw, cb, wdown, g, b, alpha, pad, lp):
    t, d = h.shape
    tm = ROW_BLOCK
    nb = lp // tm
    ck = wdown.shape[1]
    row = lambda i: (i, 0)
    halo = lambda i: (jnp.maximum(i * (tm // SUBLANES) - 1, 0), 0)
    return pl.pallas_call(
        functools.partial(_ffn_kernel, alpha=alpha, pad=pad, nb=nb),
        grid=(t // tm,),
        in_specs=[
            pl.BlockSpec((tm, d), row), pl.BlockSpec((SUBLANES, d), halo),
            _const_spec(wup.shape), _const_spec(cw.shape), _const_spec(cb.shape), _const_spec(wdown.shape),
            _const_spec(g.shape), _const_spec(b.shape),
        ],
        out_specs=pl.BlockSpec((tm, d), row),
        out_shape=jax.ShapeDtypeStruct((t, d), F32),
        scratch_shapes=[pltpu.VMEM((tm + SUBLANES, 2 * ck), F32), pltpu.VMEM((tm, d), F32)],
        compiler_params=_params("parallel"),
        name="conv_glu_ffn",
    )(h, h, wup, cw, cb, wdown, g, b)


def _rotate_half_cols(w):
    half = w.shape[-1] // 2
    return jnp.concatenate([-w[..., half:], w[..., :half]], axis=-1)


def _interleave_chunks(a, ck):
    f = a.shape[-1] // 2
    gate = a[..., :f].reshape(a.shape[:-1] + (f // ck, ck))
    val = a[..., f:].reshape(a.shape[:-1] + (f // ck, ck))
    both = jnp.concatenate([gate, val], axis=-1)
    return jnp.moveaxis(both, -2, 0)


def kernel(x, meta_tokens, emb_ln_g, emb_ln_b, w_in, q_norm_g, w_q_b, kv_norm_g, w_kv_b, w_o_attn,
           ssd_conv_w, ssd_conv_b, dt_bias, a_log, d_skip, ssd_norm_g, w_o_ssd, w_out, ln1_g, ln1_b,
           w_up, ffn_conv_w, ffn_conv_b, w_down, ln2_g, ln2_b):
    bsz, seq, d = x.shape
    depth = w_in.shape[0]
    n_meta = meta_tokens.shape[0]
    length = seq + n_meta
    pad = (-length) % ROW_ALIGN
    lp = length + pad
    assert seq % ROW_ALIGN == 0 and pad + n_meta == ROW_ALIGN and lp % ROW_BLOCK == 0
    t = bsz * lp
    alpha = float((2 * depth) ** 0.25)
    inner = ssd_norm_g.shape[1]
    n_heads = dt_bias.shape[1]
    assert inner == n_heads * SSD_HEAD_DIM and n_heads <= LANES
    conv_dim = inner + 2 * SSD_GROUPS * SSD_STATE
    d_ff = w_down.shape[1]
    assert d_ff % FF_CHUNK == 0
    qscale = float((QK_NOPE + QK_ROPE) ** -0.5 * math.log2(math.e))

    inv_freq = 1.0 / (ROPE_THETA ** (jnp.arange(0, QK_ROPE, 2, dtype=F32) / QK_ROPE))
    pos = jnp.maximum(jnp.arange(lp, dtype=jnp.int32) - pad, 0).astype(F32)
    ang = pos[:, None] * inv_freq[None, :]
    ang = jnp.concatenate([ang, ang], axis=-1)
    zpad = jnp.zeros((lp, LANES - QK_ROPE), F32)
    cos_t = jnp.concatenate([jnp.cos(ang), zpad], axis=-1)
    sin_t = jnp.concatenate([jnp.sin(ang), zpad], axis=-1)

    head_of = jnp.arange(inner, dtype=jnp.int32) // SSD_HEAD_DIM
    expand = (jnp.arange(LANES, dtype=jnp.int32)[:, None] == head_of[None, :]).astype(BF16)
    expand = jnp.concatenate([expand, expand], axis=0)

    h = _embed(x, meta_tokens, emb_ln_g[None], emb_ln_b[None], pad, lp).reshape(t, d)

    o0 = 0
    offs = []
    for n in (Q_LORA, KV_LORA, QK_ROPE, inner, conv_dim, n_heads, d, d):
        offs.append((o0, o0 + n))
        o0 += n
    s_q, s_kv, s_pe, s_z, s_xbc, s_dt, s_ga, s_gs = offs
    for li in range(depth):
        wi = w_in[li]
        col = lambda s: wi[:, s[0]:s[1]]
        w_pe = col(s_pe)
        wa = jnp.concatenate([col(s_q), col(s_kv), w_pe, _rotate_half_cols(w_pe)], axis=1).astype(BF16)
        dt_cols = jnp.pad(col(s_dt), ((0, 0), (0, LANES - n_heads)))
        wb = jnp.concatenate([dt_cols, col(s_z), col(s_xbc), col(s_ga), col(s_gs)], axis=1).astype(BF16)
        wq3 = w_q_b[li].reshape(Q_LORA, MLA_HEADS, QK_NOPE + QK_ROPE)
        wq_pe = wq3[..., QK_NOPE:]
        wq = jnp.concatenate([wq3[..., :QK_NOPE], wq_pe, _rotate_half_cols(wq_pe)], axis=-1)
        wq = wq.reshape(Q_LORA, MLA_HEADS * HEAD_SLOT).astype(BF16)
        wkv3 = w_kv_b[li].reshape(KV_LORA, MLA_HEADS, QK_NOPE + V_HEAD)
        wkv = jnp.concatenate([wkv3[..., :QK_NOPE].reshape(KV_LORA, -1),
                               wkv3[..., QK_NOPE:].reshape(KV_LORA, -1)], axis=1).astype(BF16)

        qh, kh, vh = _attn_front(h, wa, q_norm_g[li][None], wq, kv_norm_g[li][None], wkv,
                                 cos_t, sin_t, pad, lp, qscale)
        o = _attention(qh, kh, vh, bsz, lp)

        dt_raw, z, xbc, ga, gs = _proj_rest(h, wb, (LANES, inner, conv_dim, d, d))
        pad_lane = lambda v: jnp.pad(v, (0, LANES - n_heads))[None]
        y = _ssd(xbc, dt_raw, z, ssd_conv_w[li], ssd_conv_b[li][None], pad_lane(dt_bias[li]),
                 pad_lane(a_log[li]), jnp.repeat(d_skip[li], SSD_HEAD_DIM)[None], ssd_norm_g[li][None],
                 expand, bsz, lp, pad)

        h = _mix(o, y, ga, gs, h, w_o_attn[li].astype(BF16), w_o_ssd[li].astype(BF16),
                 w_out[li].astype(BF16), ln1_g[li][None], ln1_b[li][None], alpha, pad, lp)

        wup = _interleave_chunks(w_up[li], FF_CHUNK).astype(BF16)
        cw = _interleave_chunks(ffn_conv_w[li], FF_CHUNK)
        cb = _interleave_chunks(ffn_conv_b[li][None], FF_CHUNK)
        wdown = w_down[li].reshape(d_ff // FF_CHUNK, FF_CHUNK, d).astype(BF16)
        h = _ffn(h, wup, cw, cb, wdown, ln2_g[li][None], ln2_b[li][None], alpha, pad, lp)

    return h.reshape(bsz, lp, d)[:, pad + n_meta:]
```

```python
import functools
import math

import jax
import jax.numpy as jnp
from jax import lax
from jax.experimental import pallas as pl
from jax.experimental.pallas import tpu as pltpu

F32 = jnp.float32
BF16 = jnp.bfloat16

N_META = 16
MLA_HEADS = 8
Q_LORA = 768
KV_LORA = 256
QK_NOPE = 128
QK_ROPE = 64
V_HEAD = 128
ROPE_THETA = 10000.0
SSD_HEAD_DIM = 64
SSD_GROUPS = 4
SSD_STATE = 128
SSD_CONV = 4
FFN_CONV = 3
LN_EPS = 1e-5
RMS_EPS = 1e-6
NEG_BIG = -1e30

LANES = 128
SUBLANES = 8
ROW_ALIGN = 128
HEAD_SLOT = 256
VMEM_LIMIT = 56 * 1024 * 1024

ROW_BLOCK = 640
FF_CHUNK = 256
FFN_HALO = 16


def _const_spec(shape):
    nd = len(shape)
    return pl.BlockSpec(shape, lambda *_: (0,) * nd, pipeline_mode=pl.Buffered(1))


def _params(*sem):
    return pltpu.CompilerParams(dimension_semantics=sem, vmem_limit_bytes=VMEM_LIMIT)


def _layer_norm(x, g, b):
    mu = jnp.mean(x, axis=-1, keepdims=True)
    xc = x - mu
    var = jnp.mean(xc * xc, axis=-1, keepdims=True)
    return xc * lax.rsqrt(var + LN_EPS) * g + b


def _rms_norm(x, g):
    return x * lax.rsqrt(jnp.mean(x * x, axis=-1, keepdims=True) + RMS_EPS) * g


def _silu(x):
    return x * jax.nn.sigmoid(x)


def _softplus(x):
    return jnp.maximum(x, 0.0) + jnp.log1p(jnp.exp(-jnp.abs(x)))


def _mm(a, b):
    return jnp.dot(a, b, preferred_element_type=F32)


def _embed_kernel(x_ref, meta_ref, g_ref, b_ref, o_ref, *, pad):
    j = pl.program_id(1)

    @pl.when(j == 0)
    def _():
        o_ref[0, :pad, :] = jnp.zeros((pad, o_ref.shape[2]), F32)
        o_ref[0, pad:, :] = _layer_norm(meta_ref[...], g_ref[...], b_ref[...])

    @pl.when(j > 0)
    def _():
        o_ref[0] = _layer_norm(x_ref[0], g_ref[...], b_ref[...])


def _embed(x, meta, g, b, pad, lp):
    bsz, _, d = x.shape
    nblk = lp // ROW_ALIGN
    return pl.pallas_call(
        functools.partial(_embed_kernel, pad=pad),
        grid=(bsz, nblk),
        in_specs=[
            pl.BlockSpec((1, ROW_ALIGN, d), lambda bi, j: (bi, jnp.maximum(j - 1, 0), 0)),
            _const_spec(meta.shape),
            _const_spec(g.shape),
            _const_spec(b.shape),
        ],
        out_specs=pl.BlockSpec((1, ROW_ALIGN, d), lambda bi, j: (bi, j, 0)),
        out_shape=jax.ShapeDtypeStruct((bsz, lp, d), F32),
        compiler_params=_params("parallel", "arbitrary"),
        name="embed_ln",
    )(x, meta, g, b)


def _attn_front_kernel(h_ref, wa_ref, qg_ref, wqt_ref, kvg_ref, wk_ref, wvt_ref, cos_ref, sin_ref,
                       cosc_ref, sinc_ref, qt_ref, k_ref, vt_ref, *, pad, nb, qscale):
    tm = h_ref.shape[0]
    i = pl.program_id(0)
    nt = (((1,), (1,)), ((), ()))
    hb = h_ref[...].astype(BF16)
    pa = _mm(hb, wa_ref[...])
    q_lat = pa[:, :Q_LORA]
    kv_lat = pa[:, Q_LORA:Q_LORA + KV_LORA]
    kp = pa[:, Q_LORA + KV_LORA:]

    qn = _rms_norm(q_lat, qg_ref[...]).astype(BF16)
    qf = lax.dot_general(wqt_ref[...], qn, nt, preferred_element_type=F32)
    cos_c = cosc_ref[...]
    sin_c = sinc_ref[...]
    flag_row = lax.broadcasted_iota(jnp.int32, (LANES, tm), 0) == QK_ROPE
    for hh in range(MLA_HEADS):
        base = hh * HEAD_SLOT
        qt_ref[0, base:base + QK_NOPE, :] = (qf[base:base + QK_NOPE] * qscale).astype(BF16)
        t = qf[base + QK_NOPE:base + HEAD_SLOT]
        swapped = jnp.concatenate([t[QK_ROPE:], t[:QK_ROPE]], axis=0)
        r = (t * cos_c + swapped * sin_c) * qscale
        qt_ref[0, base + QK_NOPE:base + HEAD_SLOT, :] = jnp.where(flag_row, 1.0, r).astype(BF16)

    kvn = _rms_norm(kv_lat, kvg_ref[...]).astype(BF16)
    kf = _mm(kvn, wk_ref[...])
    cos_t = cos_ref[...]
    sin_t = sin_ref[...]
    lane = lax.broadcasted_iota(jnp.int32, (tm, LANES), 1)
    rowid = lax.broadcasted_iota(jnp.int32, (tm, LANES), 0) + (i % nb) * tm
    kt = kp * cos_t + pltpu.roll(kp, QK_ROPE, 1) * sin_t
    kt = jnp.where(lane == QK_ROPE, jnp.where(rowid < pad, NEG_BIG, 0.0), kt).astype(BF16)
    for hh in range(MLA_HEADS):
        base = hh * HEAD_SLOT
        k_ref[:, base:base + QK_NOPE] = kf[:, hh * QK_NOPE:(hh + 1) * QK_NOPE].astype(BF16)
        k_ref[:, base + QK_NOPE:base + HEAD_SLOT] = kt
    vt_ref[0] = lax.dot_general(wvt_ref[...], kvn, nt, preferred_element_type=F32).astype(BF16)


def _attn_front(h, wa, qg, wqt, kvg, wk, wvt, cos_t, sin_t, cos_c, sin_c, pad, lp, qscale):
    t, d = h.shape
    tm = ROW_BLOCK
    nb = lp // tm
    qk_rows = MLA_HEADS * HEAD_SLOT
    v_rows = MLA_HEADS * V_HEAD
    row = lambda i: (i, 0)
    blk = lambda i: (i, 0, 0)
    return pl.pallas_call(
        functools.partial(_attn_front_kernel, pad=pad, nb=nb, qscale=qscale),
        grid=(t // tm,),
        in_specs=[
            pl.BlockSpec((tm, d), row),
            _const_spec(wa.shape), _const_spec(qg.shape), _const_spec(wqt.shape),
            _const_spec(kvg.shape), _const_spec(wk.shape), _const_spec(wvt.shape),
            pl.BlockSpec((tm, LANES), lambda i: (i % nb, 0)),
            pl.BlockSpec((tm, LANES), lambda i: (i % nb, 0)),
            pl.BlockSpec((LANES, tm), lambda i: (0, i % nb)),
            pl.BlockSpec((LANES, tm), lambda i: (0, i % nb)),
        ],
        out_specs=[pl.BlockSpec((1, qk_rows, tm), blk), pl.BlockSpec((tm, qk_rows), row),
                   pl.BlockSpec((1, v_rows, tm), blk)],
        out_shape=[jax.ShapeDtypeStruct((t // tm, qk_rows, tm), BF16),
                   jax.ShapeDtypeStruct((t, qk_rows), BF16),
                   jax.ShapeDtypeStruct((t // tm, v_rows, tm), BF16)],
        compiler_params=_params("parallel"),
        name="attn_front",
    )(h, wa, qg, wqt, kvg, wk, wvt, cos_t, sin_t, cos_c, sin_c)


def _attn_kernel(qt_ref, k_ref, vt_ref, o_ref, sa_scr, sb_scr, m_scr, l_scr, acc_scr):
    bq = qt_ref.shape[2]
    i = pl.program_id(2)
    tiles = [(a, min(a + HEAD_SLOT, bq)) for a in range(0, bq, HEAD_SLOT)]

    def keys(j):
        return k_ref[pl.ds(pl.multiple_of(j * bq, bq), bq), :]

    def consume(j, s_of_tile, first):
        for a, b in tiles:
            s = s_of_tile(a, b)
            vj = vt_ref[j, :, :s.shape[0]]
            mx = jnp.max(s, axis=0, keepdims=True)
            if first:
                m_new = mx
            else:
                m_prev = m_scr[:, a:b]
                m_new = jnp.maximum(m_prev, mx)
            p = jnp.exp2(s - m_new)
            ps = jnp.sum(p, axis=0, keepdims=True)
            pv = _mm(vj, p.astype(BF16))
            if first:
                l_scr[:, a:b] = ps
                acc_scr[:, a:b] = pv
            else:
                alpha = jnp.exp2(m_prev - m_new)
                l_scr[:, a:b] = alpha * l_scr[:, a:b] + ps
                acc_scr[:, a:b] = alpha * acc_scr[:, a:b] + pv
            m_scr[:, a:b] = m_new

    def diag_scores(a, b):
        kd = k_ref[pl.ds(pl.multiple_of(i * bq, bq), b), :]
        s = _mm(kd, qt_ref[0, :, a:b])
        krow = lax.broadcasted_iota(jnp.int32, s.shape, 0)
        qcol = lax.broadcasted_iota(jnp.int32, s.shape, 1) + a
        return jnp.where(krow <= qcol, s, NEG_BIG)

    sa_scr[...] = _mm(keys(0), qt_ref[0])
    consume(i, diag_scores, True)

    def pair(t, carry):
        j0 = 2 * t
        sb_scr[...] = _mm(keys(j0 + 1), qt_ref[0])
        consume(j0, lambda a, b: sa_scr[:, a:b], False)
        sa_scr[...] = _mm(keys(jnp.minimum(j0 + 2, i - 1)), qt_ref[0])
        consume(j0 + 1, lambda a, b: sb_scr[:, a:b], False)
        return carry

    lax.fori_loop(0, i // 2, pair, 0)

    @pl.when(i % 2 == 1)
    def _():
        consume(i - 1, lambda a, b: sa_scr[:, a:b], False)

    o_ref[...] = (acc_scr[...] / l_scr[...]).T.astype(o_ref.dtype)


def _attention(qt, k, vt, bsz, lp):
    bq = ROW_BLOCK
    nq = lp // bq
    t = k.shape[0]
    return pl.pallas_call(
        _attn_kernel,
        grid=(bsz, MLA_HEADS, nq),
        in_specs=[
            pl.BlockSpec((1, HEAD_SLOT, bq), lambda b, h, i: (b * nq + i, h, 0)),
            pl.BlockSpec((lp, HEAD_SLOT), lambda b, h, i: (b, h)),
            pl.BlockSpec((nq, V_HEAD, bq), lambda b, h, i: (b, h, 0)),
        ],
        out_specs=pl.BlockSpec((bq, V_HEAD), lambda b, h, i: (b * nq + i, h)),
        out_shape=jax.ShapeDtypeStruct((t, MLA_HEADS * V_HEAD), BF16),
        scratch_shapes=[pltpu.VMEM((bq, bq), F32), pltpu.VMEM((bq, bq), F32),
                        pltpu.VMEM((1, bq), F32), pltpu.VMEM((1, bq), F32), pltpu.VMEM((V_HEAD, bq), F32)],
        compiler_params=_params("parallel", "parallel", "arbitrary"),
        name="flash_attn",
    )(qt, k, vt)


def _proj_kernel(h_ref, w_ref, dt_ref, z_ref, xbc_ref, ga_ref, gs_ref):
    hb = h_ref[...].astype(BF16)
    c0 = 0
    for ref in (dt_ref, z_ref, xbc_ref, ga_ref, gs_ref):
        n = ref.shape[1]
        ref[...] = _mm(hb, w_ref[:, c0:c0 + n])
        c0 += n


def _proj_rest(h, wb, widths):
    t, d = h.shape
    tm = ROW_BLOCK // 2
    row = lambda i: (i, 0)
    return pl.pallas_call(
        _proj_kernel,
        grid=(t // tm,),
        in_specs=[pl.BlockSpec((tm, d), row), _const_spec(wb.shape)],
        out_specs=[pl.BlockSpec((tm, n), row) for n in widths],
        out_shape=[jax.ShapeDtypeStruct((t, n), F32) for n in widths],
        compiler_params=_params("parallel"),
        name="proj_rest",
    )(h, wb)


def _ssd_kernel(xbc_ref, halo_ref, dt_ref, z_ref, cw_ref, cb_ref, dtb_ref, alog_ref, dskip_ref,
                ng_ref, expand_ref, y_ref, state_scr, *, pad, inner):
    q = xbc_ref.shape[0]
    c = pl.program_id(1)
    gw = inner // SSD_GROUPS
    hpg = gw // SSD_HEAD_DIM
    n = SSD_STATE

    @pl.when(c == 0)
    def _():
        state_scr[...] = jnp.zeros(state_scr.shape, F32)

    ext = jnp.concatenate([halo_ref[...], xbc_ref[...]], axis=0)
    conv = cb_ref[...] + cw_ref[SSD_CONV - 1:SSD_CONV, :] * xbc_ref[...]
    for back in range(1, SSD_CONV):
        kk = SSD_CONV - 1 - back
        conv = conv + cw_ref[kk:kk + 1, :] * pltpu.roll(ext, back, 0)[SUBLANES:]
    rowid = lax.broadcasted_iota(jnp.int32, (q, 1), 0)
    valid = jnp.logical_or(c > 0, rowid >= pad)
    u = _silu(conv)
    xs = u[:, :inner]
    bm = u[:, inner:inner + SSD_GROUPS * n]
    cm = u[:, inner + SSD_GROUPS * n:]

    dt = jnp.where(valid, _softplus(dt_ref[...] + dtb_ref[...]), 0.0)
    a = -jnp.exp(alog_ref[...])
    d_a = dt * a
    r_i = lax.broadcasted_iota(jnp.int32, (q, q), 0)
    c_i = lax.broadcasted_iota(jnp.int32, (q, q), 1)
    causal = c_i <= r_i
    tril = jnp.where(causal, 1.0, 0.0).astype(F32)
    a_cs = jnp.dot(tril, d_a, preferred_element_type=F32, precision=lax.Precision.HIGHEST)
    a_last = a_cs[q - 1:q, :]
    a_cs_t = a_cs.T

    small = jnp.concatenate([dt, jnp.exp(a_cs), jnp.exp(a_last - a_cs)], axis=0)
    hi = small.astype(BF16)
    lo = (small - hi.astype(F32)).astype(BF16)
    wide = _mm(jnp.concatenate([hi, lo], axis=1), expand_ref[...])
    dt_x = wide[0:q]
    ea_x = wide[q:2 * q]
    ds_x = wide[2 * q:3 * q]

    xc = xs * dt_x
    xcb = xc.astype(BF16)
    xd = (xc * ds_x).astype(BF16)
    lane = lax.broadcasted_iota(jnp.int32, (q, LANES), 1)
    low_half = lane < SSD_HEAD_DIM

    for g in range(SSD_GROUPS):
        cg = cm[:, g * n:(g + 1) * n].astype(BF16)
        bg = bm[:, g * n:(g + 1) * n].astype(BF16)
        cbg = lax.dot_general(cg, bg, (((1,), (1,)), ((), ())), preferred_element_type=F32)
        h_prev = state_scr[g]
        y_off = _mm(cg, h_prev.astype(BF16))
        parts = []
        for pr in range(hpg // 2):
            e0 = g * hpg + 2 * pr
            ms = []
            for e in (e0, e0 + 1):
                diff = a_cs[:, e:e + 1] - a_cs_t[e:e + 1, :]
                ms.append((cbg * jnp.exp(jnp.where(causal, diff, NEG_BIG))).astype(BF16))
            x2 = xcb[:, e0 * SSD_HEAD_DIM:(e0 + 2) * SSD_HEAD_DIM]
            zero = jnp.zeros_like(x2)
            rhs = jnp.concatenate([jnp.where(low_half, x2, zero), jnp.where(low_half, zero, x2)], axis=0)
            parts.append(_mm(jnp.concatenate(ms, axis=1), rhs))
        sl = slice(g * gw, (g + 1) * gw)
        y = jnp.concatenate(parts, axis=1) + y_off * ea_x[:, sl]
        s_new = lax.dot_general(bg, xd[:, sl], (((0,), (0,)), ((), ())), preferred_element_type=F32)
        state_scr[g] = ea_x[q - 1:q, sl] * h_prev + s_new
        y = y + xs[:, sl] * dskip_ref[:, sl]
        y = y * _silu(z_ref[:, sl])
        y = y * lax.rsqrt(jnp.mean(y * y, axis=-1, keepdims=True) + RMS_EPS) * ng_ref[:, sl]
        y_ref[:, sl] = y.astype(y_ref.dtype)


def _ssd(xbc, dt_raw, z, conv_w, conv_b, dt_bias, a_log, dskip_x, norm_g, expand, bsz, lp, pad):
    t, cdim = xbc.shape
    inner = z.shape[1]
    q = ROW_ALIGN
    nc = lp // q
    row = lambda b, c: (b * nc + c, 0)
    halo = lambda b, c: (jnp.maximum((b * nc + c) * (q // SUBLANES) - 1, 0), 0)
    return pl.pallas_call(
        functools.partial(_ssd_kernel, pad=pad, inner=inner),
        grid=(bsz, nc),
        in_specs=[
            pl.BlockSpec((q, cdim), row),
            pl.BlockSpec((SUBLANES, cdim), halo),
            pl.BlockSpec((q, LANES), row),
            pl.BlockSpec((q, inner), row),
            _const_spec(conv_w.shape), _const_spec(conv_b.shape), _const_spec(dt_bias.shape),
            _const_spec(a_log.shape), _const_spec(dskip_x.shape), _const_spec(norm_g.shape),
            _const_spec(expand.shape),
        ],
        out_specs=pl.BlockSpec((q, inner), row),
        out_shape=jax.ShapeDtypeStruct((t, inner), BF16),
        scratch_shapes=[pltpu.VMEM((SSD_GROUPS, SSD_STATE, inner // SSD_GROUPS), F32)],
        compiler_params=_params("parallel", "arbitrary"),
        name="ssd_mixer",
    )(xbc, xbc, dt_raw, z, conv_w, conv_b, dt_bias, a_log, dskip_x, norm_g, expand)


def _mix_kernel(o_ref, y_ref, ga_ref, gs_ref, h_ref, woa_ref, wos_ref, wout_ref, g_ref, b_ref, out_ref,
                *, alpha, pad, nb):
    tm = h_ref.shape[0]
    i = pl.program_id(0)
    ya = _mm(o_ref[...], woa_ref[...])
    ys = _mm(y_ref[...], wos_ref[...])
    mixed = jax.nn.sigmoid(ga_ref[...]) * ya + jax.nn.sigmoid(gs_ref[...]) * ys
    r = alpha * h_ref[...] + _mm(mixed.astype(BF16), wout_ref[...])
    rowid = lax.broadcasted_iota(jnp.int32, (tm, 1), 0) + (i % nb) * tm
    out_ref[...] = jnp.where(rowid >= pad, _layer_norm(r, g_ref[...], b_ref[...]), 0.0)


def _mix(o, y, ga, gs, h, woa, wos, wout, g, b, alpha, pad, lp):
    t, d = h.shape
    tm = ROW_BLOCK
    nb = lp // tm
    row = lambda i: (i, 0)
    return pl.pallas_call(
        functools.partial(_mix_kernel, alpha=alpha, pad=pad, nb=nb),
        grid=(t // tm,),
        in_specs=[
            pl.BlockSpec((tm, o.shape[1]), row), pl.BlockSpec((tm, y.shape[1]), row),
            pl.BlockSpec((tm, d), row), pl.BlockSpec((tm, d), row), pl.BlockSpec((tm, d), row),
            _const_spec(woa.shape), _const_spec(wos.shape), _const_spec(wout.shape),
            _const_spec(g.shape), _const_spec(b.shape),
        ],
        out_specs=pl.BlockSpec((tm, d), row),
        out_shape=jax.ShapeDtypeStruct((t, d), F32),
        compiler_params=_params("parallel"),
        name="mix_ln",
    )(o, y, ga, gs, h, woa, wos, wout, g, b)


def _ffn_kernel(h_ref, halo_ref, wup_ref, cw_ref, cb_ref, wdown_ref, g_ref, b_ref, out_ref,
                hx_scr, ua_scr, ub_scr, acc_scr, *, alpha, pad, nb):
    tm = h_ref.shape[0]
    i = pl.program_id(0)
    nchunk = wup_ref.shape[0]
    ck = wdown_ref.shape[1]
    hx_scr[0:FFN_HALO, :] = halo_ref[...].astype(BF16)
    hx_scr[FFN_HALO:, :] = h_ref[...].astype(BF16)

    def up(c, u_scr):
        u_scr[...] = _mm(hx_scr[...], wup_ref[c])

    def down(c, u_scr):
        cw = cw_ref[c]
        u = u_scr[...]
        v = cb_ref[c] + cw[FFN_CONV - 1:FFN_CONV, :] * u[FFN_HALO:]
        for back in range(1, FFN_CONV):
            kk = FFN_CONV - 1 - back
            v = v + cw[kk:kk + 1, :] * pltpu.roll(u, back, 0)[FFN_HALO:]
        act = (_silu(v[:, :ck]) * v[:, ck:]).astype(BF16)
        acc_scr[...] += _mm(act, wdown_ref[c])

    acc_scr[...] = jnp.zeros(acc_scr.shape, F32)
    up(0, ua_scr)

    def pair(t, carry):
        c = 2 * t
        up(c + 1, ub_scr)
        down(c, ua_scr)
        up(jnp.minimum(c + 2, nchunk - 1), ua_scr)
        down(c + 1, ub_scr)
        return carry

    lax.fori_loop(0, nchunk // 2, pair, 0)
    if nchunk % 2 == 1:
        down(nchunk - 1, ua_scr)
    r = alpha * h_ref[...] + acc_scr[...]
    rowid = lax.broadcasted_iota(jnp.int32, (tm, 1), 0) + (i % nb) * tm
    out_ref[...] = jnp.where(rowid >= pad, _layer_norm(r, g_ref[...], b_ref[...]), 0.0)


def _ffn(h, wup, cw, cb, wdown, g, b, alpha, pad, lp):
    t, d = h.shape
    tm = ROW_BLOCK
    nb = lp // tm
    ck = wdown.shape[1]
    row = lambda i: (i, 0)
    halo = lambda i: (jnp.maximum(i * (tm // FFN_HALO) - 1, 0), 0)
    return pl.pallas_call(
        functools.partial(_ffn_kernel, alpha=alpha, pad=pad, nb=nb),
        grid=(t // tm,),
        in_specs=[
            pl.BlockSpec((tm, d), row), pl.BlockSpec((FFN_HALO, d), halo),
            _const_spec(wup.shape), _const_spec(cw.shape), _const_spec(cb.shape), _const_spec(wdown.shape),
            _const_spec(g.shape), _const_spec(b.shape),
        ],
        out_specs=pl.BlockSpec((tm, d), row),
        out_shape=jax.ShapeDtypeStruct((t, d), F32),
        scratch_shapes=[pltpu.VMEM((tm + FFN_HALO, d), BF16), pltpu.VMEM((tm + FFN_HALO, 2 * ck), F32),
                        pltpu.VMEM((tm + FFN_HALO, 2 * ck), F32), pltpu.VMEM((tm, d), F32)],
        compiler_params=_params("parallel"),
        name="conv_glu_ffn",
    )(h, h, wup, cw, cb, wdown, g, b)


def _rotate_half_cols(w):
    half = w.shape[-1] // 2
    return jnp.concatenate([-w[..., half:], w[..., :half]], axis=-1)


def _interleave_chunks(a, ck):
    f = a.shape[-1] // 2
    gate = a[..., :f].reshape(a.shape[:-1] + (f // ck, ck))
    val = a[..., f:].reshape(a.shape[:-1] + (f // ck, ck))
    both = jnp.concatenate([gate, val], axis=-1)
    return jnp.moveaxis(both, -2, 0)


def kernel(x, meta_tokens, emb_ln_g, emb_ln_b, w_in, q_norm_g, w_q_b, kv_norm_g, w_kv_b, w_o_attn,
           ssd_conv_w, ssd_conv_b, dt_bias, a_log, d_skip, ssd_norm_g, w_o_ssd, w_out, ln1_g, ln1_b,
           w_up, ffn_conv_w, ffn_conv_b, w_down, ln2_g, ln2_b):
    bsz, seq, d = x.shape
    depth = w_in.shape[0]
    n_meta = meta_tokens.shape[0]
    length = seq + n_meta
    pad = (-length) % ROW_ALIGN
    lp = length + pad
    assert seq % ROW_ALIGN == 0 and pad + n_meta == ROW_ALIGN and lp % ROW_BLOCK == 0
    t = bsz * lp
    alpha = float((2 * depth) ** 0.25)
    inner = ssd_norm_g.shape[1]
    n_heads = dt_bias.shape[1]
    assert inner == n_heads * SSD_HEAD_DIM and n_heads <= LANES
    conv_dim = inner + 2 * SSD_GROUPS * SSD_STATE
    d_ff = w_down.shape[1]
    assert d_ff % FF_CHUNK == 0
    qscale = float((QK_NOPE + QK_ROPE) ** -0.5 * math.log2(math.e))

    inv_freq = 1.0 / (ROPE_THETA ** (jnp.arange(0, QK_ROPE, 2, dtype=F32) / QK_ROPE))
    pos = jnp.maximum(jnp.arange(lp, dtype=jnp.int32) - pad, 0).astype(F32)
    ang = pos[:, None] * inv_freq[None, :]
    ang = jnp.concatenate([ang, ang], axis=-1)
    zpad = jnp.zeros((lp, LANES - QK_ROPE), F32)
    cos_t = jnp.concatenate([jnp.cos(ang), zpad], axis=-1)
    sin_t = jnp.concatenate([jnp.sin(ang), zpad], axis=-1)
    cos_c = cos_t.T
    sin_c = sin_t.T

    head_of = jnp.arange(inner, dtype=jnp.int32) // SSD_HEAD_DIM
    expand = (jnp.arange(LANES, dtype=jnp.int32)[:, None] == head_of[None, :]).astype(BF16)
    expand = jnp.concatenate([expand, expand], axis=0)

    h = _embed(x, meta_tokens, emb_ln_g[None], emb_ln_b[None], pad, lp).reshape(t, d)

    o0 = 0
    offs = []
    for n in (Q_LORA, KV_LORA, QK_ROPE, inner, conv_dim, n_heads, d, d):
        offs.append((o0, o0 + n))
        o0 += n
    s_q, s_kv, s_pe, s_z, s_xbc, s_dt, s_ga, s_gs = offs
    for li in range(depth):
        wi = w_in[li]
        col = lambda s: wi[:, s[0]:s[1]]
        w_pe = col(s_pe)
        wa = jnp.concatenate([col(s_q), col(s_kv), w_pe, _rotate_half_cols(w_pe)], axis=1).astype(BF16)
        dt_cols = jnp.pad(col(s_dt), ((0, 0), (0, LANES - n_heads)))
        wb = jnp.concatenate([dt_cols, col(s_z), col(s_xbc), col(s_ga), col(s_gs)], axis=1).astype(BF16)
        wq3 = w_q_b[li].reshape(Q_LORA, MLA_HEADS, QK_NOPE + QK_ROPE)
        wq_pe = wq3[..., QK_NOPE:]
        wq = jnp.concatenate([wq3[..., :QK_NOPE], wq_pe, _rotate_half_cols(wq_pe)], axis=-1)
        wqt = wq.reshape(Q_LORA, MLA_HEADS * HEAD_SLOT).T.astype(BF16)
        wkv3 = w_kv_b[li].reshape(KV_LORA, MLA_HEADS, QK_NOPE + V_HEAD)
        wk = wkv3[..., :QK_NOPE].reshape(KV_LORA, -1).astype(BF16)
        wvt = wkv3[..., QK_NOPE:].reshape(KV_LORA, -1).T.astype(BF16)

        qt, kh, vt = _attn_front(h, wa, q_norm_g[li][None], wqt, kv_norm_g[li][None], wk, wvt,
                                 cos_t, sin_t, cos_c, sin_c, pad, lp, qscale)
        o = _attention(qt, kh, vt, bsz, lp)

        dt_raw, z, xbc, ga, gs = _proj_rest(h, wb, (LANES, inner, conv_dim, d, d))
        pad_lane = lambda v: jnp.pad(v, (0, LANES - n_heads))[None]
        y = _ssd(xbc, dt_raw, z, ssd_conv_w[li], ssd_conv_b[li][None], pad_lane(dt_bias[li]),
                 pad_lane(a_log[li]), jnp.repeat(d_skip[li], SSD_HEAD_DIM)[None], ssd_norm_g[li][None],
                 expand, bsz, lp, pad)

        h = _mix(o, y, ga, gs, h, w_o_attn[li].astype(BF16), w_o_ssd[li].astype(BF16),
                 w_out[li].astype(BF16), ln1_g[li][None], ln1_b[li][None], alpha, pad, lp)

        wup = _interleave_chunks(w_up[li], FF_CHUNK).astype(BF16)
        cw = _interleave_chunks(ffn_conv_w[li], FF_CHUNK)
        cb = _interleave_chunks(ffn_conv_b[li][None], FF_CHUNK)
        wdown = w_down[li].reshape(d_ff // FF_CHUNK, FF_CHUNK, d).astype(BF16)
        h = _ffn(h, wup, cw, cb, wdown, ln2_g[li][None], ln2_b[li][None], alpha, pad, lp)

    return h.reshape(bsz, lp, d)[:, pad + n_meta:]
```

```python
import functools
import math

import jax
import jax.numpy as jnp
from jax import lax
from jax.experimental import pallas as pl
from jax.experimental.pallas import tpu as pltpu

F32 = jnp.float32
BF16 = jnp.bfloat16

N_META = 16
MLA_HEADS = 8
Q_LORA = 768
KV_LORA = 256
QK_NOPE = 128
QK_ROPE = 64
V_HEAD = 128
ROPE_THETA = 10000.0
SSD_HEAD_DIM = 64
SSD_GROUPS = 4
SSD_STATE = 128
SSD_CONV = 4
FFN_CONV = 3
LN_EPS = 1e-5
RMS_EPS = 1e-6
NEG_BIG = -1e30
LOG2_E = math.log2(math.e)

LANES = 128
SUBLANES = 8
ROW_ALIGN = 128
HEAD_SLOT = 256
VMEM_LIMIT = 56 * 1024 * 1024

ROW_BLOCK = 640
FF_CHUNK = 256
FFN_HALO = 16


def _const_spec(shape):
    nd = len(shape)
    return pl.BlockSpec(shape, lambda *_: (0,) * nd, pipeline_mode=pl.Buffered(1))


def _params(*sem):
    return pltpu.CompilerParams(dimension_semantics=sem, vmem_limit_bytes=VMEM_LIMIT)


def _layer_norm(x, g, b):
    mu = jnp.mean(x, axis=-1, keepdims=True)
    xc = x - mu
    var = jnp.mean(xc * xc, axis=-1, keepdims=True)
    return xc * lax.rsqrt(var + LN_EPS) * g + b


def _rms_norm(x, g):
    return x * lax.rsqrt(jnp.mean(x * x, axis=-1, keepdims=True) + RMS_EPS) * g


def _silu(x):
    half = 0.5 * x
    return half + half * jnp.tanh(half)


def _softplus(x):
    return jnp.maximum(x, 0.0) + jnp.log1p(jnp.exp(-jnp.abs(x)))


def _mm(a, b):
    return jnp.dot(a, b, preferred_element_type=F32)


def _embed_kernel(*refs, pad, nsub):
    x_refs = refs[:nsub]
    meta_ref, g_ref, b_ref, o_ref = refs[nsub:]
    j = pl.program_id(1)
    for s, x_ref in enumerate(x_refs):
        rows = slice(s * ROW_ALIGN, (s + 1) * ROW_ALIGN)
        if s == 0:
            @pl.when(j == 0)
            def _():
                o_ref[0, :pad, :] = jnp.zeros((pad, o_ref.shape[2]), F32)
                o_ref[0, pad:ROW_ALIGN, :] = _layer_norm(meta_ref[...], g_ref[...], b_ref[...])

            @pl.when(j > 0)
            def _():
                o_ref[0, rows, :] = _layer_norm(x_ref[0], g_ref[...], b_ref[...])
        else:
            o_ref[0, rows, :] = _layer_norm(x_ref[0], g_ref[...], b_ref[...])


def _embed(x, meta, g, b, pad, lp):
    bsz, _, d = x.shape
    tm = ROW_BLOCK
    nsub = tm // ROW_ALIGN

    def x_spec(s):
        return pl.BlockSpec((1, ROW_ALIGN, d), lambda bi, j: (bi, jnp.maximum(j * nsub + s - 1, 0), 0))

    return pl.pallas_call(
        functools.partial(_embed_kernel, pad=pad, nsub=nsub),
        grid=(bsz, lp // tm),
        in_specs=[x_spec(s) for s in range(nsub)] + [
            _const_spec(meta.shape),
            _const_spec(g.shape),
            _const_spec(b.shape),
        ],
        out_specs=pl.BlockSpec((1, tm, d), lambda bi, j: (bi, j, 0)),
        out_shape=jax.ShapeDtypeStruct((bsz, lp, d), F32),
        compiler_params=_params("parallel", "arbitrary"),
        name="embed_ln",
    )(*([x] * nsub), meta, g, b)


def _attn_front_kernel(h_ref, wa_ref, qg_ref, wqt_ref, kvg_ref, wk_ref, wvt_ref, cos_ref, sin_ref,
                       cosc_ref, sinc_ref, qt_ref, k_ref, vt_ref, *, pad, nb, qscale):
    tm = h_ref.shape[0]
    i = pl.program_id(0)
    nt = (((1,), (1,)), ((), ()))
    hb = h_ref[...].astype(BF16)
    pa = _mm(hb, wa_ref[...])
    q_lat = pa[:, :Q_LORA]
    kv_lat = pa[:, Q_LORA:Q_LORA + KV_LORA]
    kp = pa[:, Q_LORA + KV_LORA:]

    qn = _rms_norm(q_lat, qg_ref[...]).astype(BF16)
    qf = lax.dot_general(wqt_ref[...], qn, nt, preferred_element_type=F32)
    cos_c = cosc_ref[...]
    sin_c = sinc_ref[...]
    flag_row = lax.broadcasted_iota(jnp.int32, (LANES, tm), 0) == QK_ROPE
    for hh in range(MLA_HEADS):
        base = hh * HEAD_SLOT
        qt_ref[0, base:base + QK_NOPE, :] = (qf[base:base + QK_NOPE] * qscale).astype(BF16)
        t = qf[base + QK_NOPE:base + HEAD_SLOT]
        swapped = jnp.concatenate([t[QK_ROPE:], t[:QK_ROPE]], axis=0)
        r = (t * cos_c + swapped * sin_c) * qscale
        qt_ref[0, base + QK_NOPE:base + HEAD_SLOT, :] = jnp.where(flag_row, 1.0, r).astype(BF16)

    kvn = _rms_norm(kv_lat, kvg_ref[...]).astype(BF16)
    kf = _mm(kvn, wk_ref[...])
    cos_t = cos_ref[...]
    sin_t = sin_ref[...]
    lane = lax.broadcasted_iota(jnp.int32, (tm, LANES), 1)
    rowid = lax.broadcasted_iota(jnp.int32, (tm, LANES), 0) + (i % nb) * tm
    kt = kp * cos_t + pltpu.roll(kp, QK_ROPE, 1) * sin_t
    kt = jnp.where(lane == QK_ROPE, jnp.where(rowid < pad, NEG_BIG, 0.0), kt).astype(BF16)
    for hh in range(MLA_HEADS):
        base = hh * HEAD_SLOT
        k_ref[:, base:base + QK_NOPE] = kf[:, hh * QK_NOPE:(hh + 1) * QK_NOPE].astype(BF16)
        k_ref[:, base + QK_NOPE:base + HEAD_SLOT] = kt
    vt_ref[0] = lax.dot_general(wvt_ref[...], kvn, nt, preferred_element_type=F32).astype(BF16)


def _attn_front(h, wa, qg, wqt, kvg, wk, wvt, cos_t, sin_t, cos_c, sin_c, pad, lp, qscale):
    t, d = h.shape
    tm = ROW_BLOCK
    nb = lp // tm
    qk_rows = MLA_HEADS * HEAD_SLOT
    v_rows = MLA_HEADS * V_HEAD
    row = lambda i: (i, 0)
    blk = lambda i: (i, 0, 0)
    return pl.pallas_call(
        functools.partial(_attn_front_kernel, pad=pad, nb=nb, qscale=qscale),
        grid=(t // tm,),
        in_specs=[
            pl.BlockSpec((tm, d), row),
            _const_spec(wa.shape), _const_spec(qg.shape), _const_spec(wqt.shape),
            _const_spec(kvg.shape), _const_spec(wk.shape), _const_spec(wvt.shape),
            pl.BlockSpec((tm, LANES), lambda i: (i % nb, 0)),
            pl.BlockSpec((tm, LANES), lambda i: (i % nb, 0)),
            pl.BlockSpec((LANES, tm), lambda i: (0, i % nb)),
            pl.BlockSpec((LANES, tm), lambda i: (0, i % nb)),
        ],
        out_specs=[pl.BlockSpec((1, qk_rows, tm), blk), pl.BlockSpec((tm, qk_rows), row),
                   pl.BlockSpec((1, v_rows, tm), blk)],
        out_shape=[jax.ShapeDtypeStruct((t // tm, qk_rows, tm), BF16),
                   jax.ShapeDtypeStruct((t, qk_rows), BF16),
                   jax.ShapeDtypeStruct((t // tm, v_rows, tm), BF16)],
        compiler_params=_params("parallel"),
        name="attn_front",
    )(h, wa, qg, wqt, kvg, wk, wvt, cos_t, sin_t, cos_c, sin_c)


def _attn_kernel(qt_ref, k_ref, vt_ref, o_ref, sa_scr, sb_scr, m_scr, l_scr, acc_scr):
    bq = qt_ref.shape[2]
    i = pl.program_id(2)
    tiles = [(a, min(a + HEAD_SLOT, bq)) for a in range(0, bq, HEAD_SLOT)]

    def keys(j):
        return k_ref[pl.ds(pl.multiple_of(j * bq, bq), bq), :]

    def consume(j, s_scr, diagonal=False):
        for a, b in tiles:
            if diagonal:
                s = s_scr[:b, a:b]
                krow = lax.broadcasted_iota(jnp.int32, s.shape, 0)
                qcol = lax.broadcasted_iota(jnp.int32, s.shape, 1) + a
                s = jnp.where(krow <= qcol, s, NEG_BIG)
            else:
                s = s_scr[:, a:b]
            vj = vt_ref[j, :, :s.shape[0]]
            m_prev = m_scr[:, a:b]
            m_new = jnp.maximum(m_prev, jnp.max(s, axis=0, keepdims=True))
            p = jnp.exp2(s - m_new)
            alpha = jnp.exp2(m_prev - m_new)
            l_scr[:, a:b] = alpha * l_scr[:, a:b] + jnp.sum(p, axis=0, keepdims=True)
            acc_scr[:, a:b] = alpha * acc_scr[:, a:b] + _mm(vj, p.astype(BF16))
            m_scr[:, a:b] = m_new

    def scores(j, s_scr):
        s_scr[...] = _mm(keys(j), qt_ref[0])

    m_scr[...] = jnp.full(m_scr.shape, NEG_BIG, F32)
    l_scr[...] = jnp.zeros(l_scr.shape, F32)
    acc_scr[...] = jnp.zeros(acc_scr.shape, F32)
    scores(0, sa_scr)

    def pair(t, carry):
        j0 = 2 * t
        scores(j0 + 1, sb_scr)
        consume(j0, sa_scr)
        scores(j0 + 2, sa_scr)
        consume(j0 + 1, sb_scr)
        return carry

    lax.fori_loop(0, i // 2, pair, 0)

    @pl.when(i % 2 == 0)
    def _():
        consume(i, sa_scr, diagonal=True)

    @pl.when(i % 2 == 1)
    def _():
        scores(i, sb_scr)
        consume(i - 1, sa_scr)
        consume(i, sb_scr, diagonal=True)

    o_ref[...] = (acc_scr[...] / l_scr[...]).T.astype(o_ref.dtype)


def _attention(qt, k, vt, bsz, lp):
    bq = ROW_BLOCK
    nq = lp // bq
    t = k.shape[0]
    return pl.pallas_call(
        _attn_kernel,
        grid=(bsz, MLA_HEADS, nq),
        in_specs=[
            pl.BlockSpec((1, HEAD_SLOT, bq), lambda b, h, i: (b * nq + i, h, 0)),
            pl.BlockSpec((lp, HEAD_SLOT), lambda b, h, i: (b, h)),
            pl.BlockSpec((nq, V_HEAD, bq), lambda b, h, i: (b, h, 0)),
        ],
        out_specs=pl.BlockSpec((bq, V_HEAD), lambda b, h, i: (b * nq + i, h)),
        out_shape=jax.ShapeDtypeStruct((t, MLA_HEADS * V_HEAD), BF16),
        scratch_shapes=[pltpu.VMEM((bq, bq), F32), pltpu.VMEM((bq, bq), F32),
                        pltpu.VMEM((1, bq), F32), pltpu.VMEM((1, bq), F32), pltpu.VMEM((V_HEAD, bq), F32)],
        compiler_params=_params("parallel", "parallel", "arbitrary"),
        name="flash_attn",
    )(qt, k, vt)


def _proj_kernel(h_ref, w_ref, dt_ref, z_ref, xbc_ref, ga_ref, gs_ref):
    hb = h_ref[...].astype(BF16)
    c0 = 0
    for ref in (dt_ref, z_ref, xbc_ref, ga_ref, gs_ref):
        n = ref.shape[1]
        ref[...] = _mm(hb, w_ref[:, c0:c0 + n])
        c0 += n


def _proj_rest(h, wb, widths):
    t, d = h.shape
    tm = ROW_BLOCK // 2
    row = lambda i: (i, 0)
    return pl.pallas_call(
        _proj_kernel,
        grid=(t // tm,),
        in_specs=[pl.BlockSpec((tm, d), row), _const_spec(wb.shape)],
        out_specs=[pl.BlockSpec((tm, n), row) for n in widths],
        out_shape=[jax.ShapeDtypeStruct((t, n), F32) for n in widths],
        compiler_params=_params("parallel"),
        name="proj_rest",
    )(h, wb)


def _ssd_kernel(xbc_ref, halo_ref, dt_ref, z_ref, cw_ref, cb_ref, dtb_ref, alog_ref, dskip_ref,
                ng_ref, expand_ref, y_ref, state_scr, *, pad, inner):
    q = xbc_ref.shape[0]
    c = pl.program_id(1)
    gw = inner // SSD_GROUPS
    hpg = gw // SSD_HEAD_DIM
    n = SSD_STATE

    @pl.when(c == 0)
    def _():
        state_scr[...] = jnp.zeros(state_scr.shape, F32)

    ext = jnp.concatenate([halo_ref[...], xbc_ref[...]], axis=0)
    conv = cb_ref[...] + cw_ref[SSD_CONV - 1:SSD_CONV, :] * xbc_ref[...]
    for back in range(1, SSD_CONV):
        kk = SSD_CONV - 1 - back
        conv = conv + cw_ref[kk:kk + 1, :] * pltpu.roll(ext, back, 0)[SUBLANES:]
    rowid = lax.broadcasted_iota(jnp.int32, (q, 1), 0)
    valid = jnp.logical_or(c > 0, rowid >= pad)
    u = _silu(conv)
    xs = u[:, :inner]
    bm = u[:, inner:inner + SSD_GROUPS * n]
    cm = u[:, inner + SSD_GROUPS * n:]

    dt = jnp.where(valid, _softplus(dt_ref[...] + dtb_ref[...]), 0.0)
    a = -jnp.exp(alog_ref[...])
    d_a = dt * a
    r_i = lax.broadcasted_iota(jnp.int32, (q, q), 0)
    c_i = lax.broadcasted_iota(jnp.int32, (q, q), 1)
    causal = c_i <= r_i
    tril = jnp.where(causal, 1.0, 0.0).astype(F32)
    a_cs = jnp.dot(tril, d_a, preferred_element_type=F32, precision=lax.Precision.HIGHEST)
    a_last = a_cs[q - 1:q, :]
    a_cs2 = a_cs * LOG2_E
    a_cs2_t = a_cs2.T

    small = jnp.concatenate([dt, jnp.exp(a_cs), jnp.exp(a_last - a_cs)], axis=0)
    hi = small.astype(BF16)
    lo = (small - hi.astype(F32)).astype(BF16)
    wide = _mm(jnp.concatenate([hi, lo], axis=1), expand_ref[...])
    dt_x = wide[0:q]
    ea_x = wide[q:2 * q]
    ds_x = wide[2 * q:3 * q]

    xc = xs * dt_x
    xcb = xc.astype(BF16)
    xd = (xc * ds_x).astype(BF16)
    lane = lax.broadcasted_iota(jnp.int32, (q, LANES), 1)
    low_half = lane < SSD_HEAD_DIM

    for g in range(SSD_GROUPS):
        cg = cm[:, g * n:(g + 1) * n].astype(BF16)
        bg = bm[:, g * n:(g + 1) * n].astype(BF16)
        cbg = lax.dot_general(cg, bg, (((1,), (1,)), ((), ())), preferred_element_type=F32)
        h_prev = state_scr[g]
        y_off = _mm(cg, h_prev.astype(BF16))
        parts = []
        for pr in range(hpg // 2):
            e0 = g * hpg + 2 * pr
            ms = []
            for e in (e0, e0 + 1):
                diff = a_cs2[:, e:e + 1] - a_cs2_t[e:e + 1, :]
                ms.append((cbg * jnp.exp2(jnp.where(causal, diff, NEG_BIG))).astype(BF16))
            x2 = xcb[:, e0 * SSD_HEAD_DIM:(e0 + 2) * SSD_HEAD_DIM]
            zero = jnp.zeros_like(x2)
            rhs = jnp.concatenate([jnp.where(low_half, x2, zero), jnp.where(low_half, zero, x2)], axis=0)
            parts.append(_mm(jnp.concatenate(ms, axis=1), rhs))
        sl = slice(g * gw, (g + 1) * gw)
        y = jnp.concatenate(parts, axis=1) + y_off * ea_x[:, sl]
        s_new = lax.dot_general(bg, xd[:, sl], (((0,), (0,)), ((), ())), preferred_element_type=F32)
        state_scr[g] = ea_x[q - 1:q, sl] * h_prev + s_new
        y = y + xs[:, sl] * dskip_ref[:, sl]
        y = y * _silu(z_ref[:, sl])
        y = y * lax.rsqrt(jnp.mean(y * y, axis=-1, keepdims=True) + RMS_EPS) * ng_ref[:, sl]
        y_ref[:, sl] = y.astype(y_ref.dtype)


def _ssd(xbc, dt_raw, z, conv_w, conv_b, dt_bias, a_log, dskip_x, norm_g, expand, bsz, lp, pad):
    t, cdim = xbc.shape
    inner = z.shape[1]
    q = ROW_ALIGN
    nc = lp // q
    row = lambda b, c: (b * nc + c, 0)
    halo = lambda b, c: (jnp.maximum((b * nc + c) * (q // SUBLANES) - 1, 0), 0)
    return pl.pallas_call(
        functools.partial(_ssd_kernel, pad=pad, inner=inner),
        grid=(bsz, nc),
        in_specs=[
            pl.BlockSpec((q, cdim), row),
            pl.BlockSpec((SUBLANES, cdim), halo),
            pl.BlockSpec((q, LANES), row),
            pl.BlockSpec((q, inner), row),
            _const_spec(conv_w.shape), _const_spec(conv_b.shape), _const_spec(dt_bias.shape),
            _const_spec(a_log.shape), _const_spec(dskip_x.shape), _const_spec(norm_g.shape),
            _const_spec(expand.shape),
        ],
        out_specs=pl.BlockSpec((q, inner), row),
        out_shape=jax.ShapeDtypeStruct((t, inner), BF16),
        scratch_shapes=[pltpu.VMEM((SSD_GROUPS, SSD_STATE, inner // SSD_GROUPS), F32)],
        compiler_params=_params("parallel", "arbitrary"),
        name="ssd_mixer",
    )(xbc, xbc, dt_raw, z, conv_w, conv_b, dt_bias, a_log, dskip_x, norm_g, expand)


def _mix_kernel(o_ref, y_ref, ga_ref, gs_ref, h_ref, woa_ref, wos_ref, wout_ref, g_ref, b_ref, out_ref,
                *, alpha, pad, nb):
    tm = h_ref.shape[0]
    i = pl.program_id(0)
    ya = _mm(o_ref[...], woa_ref[...])
    ys = _mm(y_ref[...], wos_ref[...])
    mixed = jax.nn.sigmoid(ga_ref[...]) * ya + jax.nn.sigmoid(gs_ref[...]) * ys
    r = alpha * h_ref[...] + _mm(mixed.astype(BF16), wout_ref[...])
    rowid = lax.broadcasted_iota(jnp.int32, (tm, 1), 0) + (i % nb) * tm
    out_ref[...] = jnp.where(rowid >= pad, _layer_norm(r, g_ref[...], b_ref[...]), 0.0)


def _mix(o, y, ga, gs, h, woa, wos, wout, g, b, alpha, pad, lp):
    t, d = h.shape
    tm = ROW_BLOCK
    nb = lp // tm
    row = lambda i: (i, 0)
    return pl.pallas_call(
        functools.partial(_mix_kernel, alpha=alpha, pad=pad, nb=nb),
        grid=(t // tm,),
        in_specs=[
            pl.BlockSpec((tm, o.shape[1]), row), pl.BlockSpec((tm, y.shape[1]), row),
            pl.BlockSpec((tm, d), row), pl.BlockSpec((tm, d), row), pl.BlockSpec((tm, d), row),
            _const_spec(woa.shape), _const_spec(wos.shape), _const_spec(wout.shape),
            _const_spec(g.shape), _const_spec(b.shape),
        ],
        out_specs=pl.BlockSpec((tm, d), row),
        out_shape=jax.ShapeDtypeStruct((t, d), F32),
        compiler_params=_params("parallel"),
        name="mix_ln",
    )(o, y, ga, gs, h, woa, wos, wout, g, b)


def _ffn_kernel(h_ref, halo_ref, wup_ref, cw_ref, cb_ref, wdown_ref, g_ref, b_ref, out_ref,
                hx_scr, ua_scr, ub_scr, acc_scr, *, alpha, pad, nb):
    tm = h_ref.shape[0]
    i = pl.program_id(0)
    nchunk = wup_ref.shape[0]
    ck = wdown_ref.shape[1]
    hx_scr[0:FFN_HALO, :] = halo_ref[...].astype(BF16)
    hx_scr[FFN_HALO:, :] = h_ref[...].astype(BF16)

    def up(c, u_scr):
        u_scr[...] = _mm(hx_scr[...], wup_ref[c])

    def down(c, u_scr):
        cw = cw_ref[c]
        u = u_scr[...]
        v = cb_ref[c] + cw[FFN_CONV - 1:FFN_CONV, :] * u[FFN_HALO:]
        for back in range(1, FFN_CONV):
            kk = FFN_CONV - 1 - back
            v = v + cw[kk:kk + 1, :] * pltpu.roll(u, back, 0)[FFN_HALO:]
        act = (_silu(v[:, :ck]) * v[:, ck:]).astype(BF16)
        acc_scr[...] += _mm(act, wdown_ref[c])

    acc_scr[...] = jnp.zeros(acc_scr.shape, F32)
    up(0, ua_scr)

    def pair(t, carry):
        c = 2 * t
        up(c + 1, ub_scr)
        down(c, ua_scr)
        up(jnp.minimum(c + 2, nchunk - 1), ua_scr)
        down(c + 1, ub_scr)
        return carry

    lax.fori_loop(0, nchunk // 2, pair, 0)
    if nchunk % 2 == 1:
        down(nchunk - 1, ua_scr)
    r = alpha * h_ref[...] + acc_scr[...]
    rowid = lax.broadcasted_iota(jnp.int32, (tm, 1), 0) + (i % nb) * tm
    out_ref[...] = jnp.where(rowid >= pad, _layer_norm(r, g_ref[...], b_ref[...]), 0.0)


def _ffn(h, wup, cw, cb, wdown, g, b, alpha, pad, lp):
    t, d = h.shape
    tm = ROW_BLOCK
    nb = lp // tm
    ck = wdown.shape[1]
    row = lambda i: (i, 0)
    halo = lambda i: (jnp.maximum(i * (tm // FFN_HALO) - 1, 0), 0)
    return pl.pallas_call(
        functools.partial(_ffn_kernel, alpha=alpha, pad=pad, nb=nb),
        grid=(t // tm,),
        in_specs=[
            pl.BlockSpec((tm, d), row), pl.BlockSpec((FFN_HALO, d), halo),
            _const_spec(wup.shape), _const_spec(cw.shape), _const_spec(cb.shape), _const_spec(wdown.shape),
            _const_spec(g.shape), _const_spec(b.shape),
        ],
        out_specs=pl.BlockSpec((tm, d), row),
        out_shape=jax.ShapeDtypeStruct((t, d), F32),
        scratch_shapes=[pltpu.VMEM((tm + FFN_HALO, d), BF16), pltpu.VMEM((tm + FFN_HALO, 2 * ck), F32),
                        pltpu.VMEM((tm + FFN_HALO, 2 * ck), F32), pltpu.VMEM((tm, d), F32)],
        compiler_params=_params("parallel"),
        name="conv_glu_ffn",
    )(h, h, wup, cw, cb, wdown, g, b)


def _rotate_half_cols(w):
    half = w.shape[-1] // 2
    return jnp.concatenate([-w[..., half:], w[..., :half]], axis=-1)


def _interleave_chunks(a, ck):
    f = a.shape[-1] // 2
    gate = a[..., :f].reshape(a.shape[:-1] + (f // ck, ck))
    val = a[..., f:].reshape(a.shape[:-1] + (f // ck, ck))
    both = jnp.concatenate([gate, val], axis=-1)
    return jnp.moveaxis(both, -2, 0)


def kernel(x, meta_tokens, emb_ln_g, emb_ln_b, w_in, q_norm_g, w_q_b, kv_norm_g, w_kv_b, w_o_attn,
           ssd_conv_w, ssd_conv_b, dt_bias, a_log, d_skip, ssd_norm_g, w_o_ssd, w_out, ln1_g, ln1_b,
           w_up, ffn_conv_w, ffn_conv_b, w_down, ln2_g, ln2_b):
    bsz, seq, d = x.shape
    depth = w_in.shape[0]
    n_meta = meta_tokens.shape[0]
    length = seq + n_meta
    pad = (-length) % ROW_ALIGN
    lp = length + pad
    assert seq % ROW_ALIGN == 0 and pad + n_meta == ROW_ALIGN and lp % ROW_BLOCK == 0
    t = bsz * lp
    alpha = float((2 * depth) ** 0.25)
    inner = ssd_norm_g.shape[1]
    n_heads = dt_bias.shape[1]
    assert inner == n_heads * SSD_HEAD_DIM and n_heads <= LANES
    conv_dim = inner + 2 * SSD_GROUPS * SSD_STATE
    d_ff = w_down.shape[1]
    assert d_ff % FF_CHUNK == 0
    qscale = float((QK_NOPE + QK_ROPE) ** -0.5 * math.log2(math.e))

    inv_freq = 1.0 / (ROPE_THETA ** (jnp.arange(0, QK_ROPE, 2, dtype=F32) / QK_ROPE))
    pos = jnp.maximum(jnp.arange(lp, dtype=jnp.int32) - pad, 0).astype(F32)
    ang = pos[:, None] * inv_freq[None, :]
    ang = jnp.concatenate([ang, ang], axis=-1)
    zpad = jnp.zeros((lp, LANES - QK_ROPE), F32)
    cos_t = jnp.concatenate([jnp.cos(ang), zpad], axis=-1)
    sin_t = jnp.concatenate([jnp.sin(ang), zpad], axis=-1)
    cos_c = cos_t.T
    sin_c = sin_t.T

    head_of = jnp.arange(inner, dtype=jnp.int32) // SSD_HEAD_DIM
    expand = (jnp.arange(LANES, dtype=jnp.int32)[:, None] == head_of[None, :]).astype(BF16)
    expand = jnp.concatenate([expand, expand], axis=0)

    h = _embed(x, meta_tokens, emb_ln_g[None], emb_ln_b[None], pad, lp).reshape(t, d)

    o0 = 0
    offs = []
    for n in (Q_LORA, KV_LORA, QK_ROPE, inner, conv_dim, n_heads, d, d):
        offs.append((o0, o0 + n))
        o0 += n
    s_q, s_kv, s_pe, s_z, s_xbc, s_dt, s_ga, s_gs = offs
    for li in range(depth):
        wi = w_in[li]
        col = lambda s: wi[:, s[0]:s[1]]
        w_pe = col(s_pe)
        wa = jnp.concatenate([col(s_q), col(s_kv), w_pe, _rotate_half_cols(w_pe)], axis=1).astype(BF16)
        dt_cols = jnp.pad(col(s_dt), ((0, 0), (0, LANES - n_heads)))
        wb = jnp.concatenate([dt_cols, col(s_z), col(s_xbc), col(s_ga), col(s_gs)], axis=1).astype(BF16)
        wq3 = w_q_b[li].reshape(Q_LORA, MLA_HEADS, QK_NOPE + QK_ROPE)
        wq_pe = wq3[..., QK_NOPE:]
        wq = jnp.concatenate([wq3[..., :QK_NOPE], wq_pe, _rotate_half_cols(wq_pe)], axis=-1)
        wqt = wq.reshape(Q_LORA, MLA_HEADS * HEAD_SLOT).T.astype(BF16)
        wkv3 = w_kv_b[li].reshape(KV_LORA, MLA_HEADS, QK_NOPE + V_HEAD)
        wk = wkv3[..., :QK_NOPE].reshape(KV_LORA, -1).astype(BF16)
        wvt = wkv3[..., QK_NOPE:].reshape(KV_LORA, -1).T.astype(BF16)

        qt, kh, vt = _attn_front(h, wa, q_norm_g[li][None], wqt, kv_norm_g[li][None], wk, wvt,
                                 cos_t, sin_t, cos_c, sin_c, pad, lp, qscale)
        o = _attention(qt, kh, vt, bsz, lp)

        dt_raw, z, xbc, ga, gs = _proj_rest(h, wb, (LANES, inner, conv_dim, d, d))
        pad_lane = lambda v: jnp.pad(v, (0, LANES - n_heads))[None]
        y = _ssd(xbc, dt_raw, z, ssd_conv_w[li], ssd_conv_b[li][None], pad_lane(dt_bias[li]),
                 pad_lane(a_log[li]), jnp.repeat(d_skip[li], SSD_HEAD_DIM)[None], ssd_norm_g[li][None],
                 expand, bsz, lp, pad)

        h = _mix(o, y, ga, gs, h, w_o_attn[li].astype(BF16), w_o_ssd[li].astype(BF16),
                 w_out[li].astype(BF16), ln1_g[li][None], ln1_b[li][None], alpha, pad, lp)

        wup = _interleave_chunks(w_up[li], FF_CHUNK).astype(BF16)
        cw = _interleave_chunks(ffn_conv_w[li], FF_CHUNK)
        cb = _interleave_chunks(ffn_conv_b[li][None], FF_CHUNK)
        wdown = w_down[li].reshape(d_ff // FF_CHUNK, FF_CHUNK, d).astype(BF16)
        h = _ffn(h, wup, cw, cb, wdown, ln2_g[li][None], ln2_b[li][None], alpha, pad, lp)

    return h.reshape(bsz, lp, d)[:, pad + n_meta:]
```

```python
import functools
import math

import jax
import jax.numpy as jnp
from jax import lax
from jax.experimental import pallas as pl
from jax.experimental.pallas import tpu as pltpu

F32 = jnp.float32
BF16 = jnp.bfloat16

N_META = 16
MLA_HEADS = 8
Q_LORA = 768
KV_LORA = 256
QK_NOPE = 128
QK_ROPE = 64
V_HEAD = 128
ROPE_THETA = 10000.0
SSD_HEAD_DIM = 64
SSD_GROUPS = 4
SSD_STATE = 128
SSD_CONV = 4
FFN_CONV = 3
LN_EPS = 1e-5
RMS_EPS = 1e-6
NEG_BIG = -1e30
LOG2_E = math.log2(math.e)

LANES = 128
SUBLANES = 8
ROW_ALIGN = 128
HEAD_SLOT = 256
VMEM_LIMIT = 56 * 1024 * 1024

ROW_BLOCK = 640
FF_CHUNK = 256
FFN_HALO = 16

def _const_spec(shape):
    nd = len(shape)
    return pl.BlockSpec(shape, lambda *_: (0,) * nd, pipeline_mode=pl.Buffered(1))


def _params(*sem):
    return pltpu.CompilerParams(dimension_semantics=sem, vmem_limit_bytes=VMEM_LIMIT)


def _layer_norm(x, g, b):
    mu = jnp.mean(x, axis=-1, keepdims=True)
    xc = x - mu
    var = jnp.mean(xc * xc, axis=-1, keepdims=True)
    return xc * lax.rsqrt(var + LN_EPS) * g + b


def _rms_norm(x, g):
    return x * lax.rsqrt(jnp.mean(x * x, axis=-1, keepdims=True) + RMS_EPS) * g


def _silu(x):
    half = 0.5 * x
    return half + half * jnp.tanh(half)


def _softplus(x):
    return jnp.maximum(x, 0.0) + jnp.log1p(jnp.exp(-jnp.abs(x)))


def _mm(a, b):
    return jnp.dot(a, b, preferred_element_type=F32)


def _embed_kernel(*refs, pad, nsub):
    x_refs = refs[:nsub]
    meta_ref, g_ref, b_ref, o_ref = refs[nsub:]
    j = pl.program_id(1)
    for s, x_ref in enumerate(x_refs):
        rows = slice(s * ROW_ALIGN, (s + 1) * ROW_ALIGN)
        if s == 0:
            @pl.when(j == 0)
            def _():
                o_ref[0, :pad, :] = jnp.zeros((pad, o_ref.shape[2]), F32)
                o_ref[0, pad:ROW_ALIGN, :] = _layer_norm(meta_ref[...], g_ref[...], b_ref[...])

            @pl.when(j > 0)
            def _():
                o_ref[0, rows, :] = _layer_norm(x_ref[0], g_ref[...], b_ref[...])
        else:
            o_ref[0, rows, :] = _layer_norm(x_ref[0], g_ref[...], b_ref[...])


def _embed(x, meta, g, b, pad, lp):
    bsz, _, d = x.shape
    tm = ROW_BLOCK
    nsub = tm // ROW_ALIGN

    def x_spec(s):
        return pl.BlockSpec((1, ROW_ALIGN, d), lambda bi, j: (bi, jnp.maximum(j * nsub + s - 1, 0), 0))

    return pl.pallas_call(
        functools.partial(_embed_kernel, pad=pad, nsub=nsub),
        grid=(bsz, lp // tm),
        in_specs=[x_spec(s) for s in range(nsub)] + [
            _const_spec(meta.shape),
            _const_spec(g.shape),
            _const_spec(b.shape),
        ],
        out_specs=pl.BlockSpec((1, tm, d), lambda bi, j: (bi, j, 0)),
        out_shape=jax.ShapeDtypeStruct((bsz, lp, d), F32),
        compiler_params=_params("parallel", "arbitrary"),
        name="embed_ln",
    )(*([x] * nsub), meta, g, b)


def _attn_front_kernel(h_ref, wa_ref, qg_ref, wqt_ref, kvg_ref, wk_ref, wvt_ref, cos_ref, sin_ref,
                       cosc_ref, sinc_ref, qt_ref, k_ref, vt_ref, *, pad, nb, qscale):
    tm = h_ref.shape[0]
    i = pl.program_id(0)
    nt = (((1,), (1,)), ((), ()))
    hb = h_ref[...].astype(BF16)
    pa = _mm(hb, wa_ref[...])
    q_lat = pa[:, :Q_LORA]
    kv_lat = pa[:, Q_LORA:Q_LORA + KV_LORA]
    kp = pa[:, Q_LORA + KV_LORA:]

    qn = _rms_norm(q_lat, qg_ref[...]).astype(BF16)
    qf = lax.dot_general(wqt_ref[...], qn, nt, preferred_element_type=F32)
    cos_c = cosc_ref[...]
    sin_c = sinc_ref[...]
    flag_row = lax.broadcasted_iota(jnp.int32, (LANES, tm), 0) == QK_ROPE
    for hh in range(MLA_HEADS):
        base = hh * HEAD_SLOT
        qt_ref[0, base:base + QK_NOPE, :] = (qf[base:base + QK_NOPE] * qscale).astype(BF16)
        t = qf[base + QK_NOPE:base + HEAD_SLOT]
        swapped = jnp.concatenate([t[QK_ROPE:], t[:QK_ROPE]], axis=0)
        r = (t * cos_c + swapped * sin_c) * qscale
        qt_ref[0, base + QK_NOPE:base + HEAD_SLOT, :] = jnp.where(flag_row, 1.0, r).astype(BF16)

    kvn = _rms_norm(kv_lat, kvg_ref[...]).astype(BF16)
    kf = _mm(kvn, wk_ref[...])
    cos_t = cos_ref[...]
    sin_t = sin_ref[...]
    lane = lax.broadcasted_iota(jnp.int32, (tm, LANES), 1)
    rowid = lax.broadcasted_iota(jnp.int32, (tm, LANES), 0) + (i % nb) * tm
    kt = kp * cos_t + pltpu.roll(kp, QK_ROPE, 1) * sin_t
    kt = jnp.where(lane == QK_ROPE, jnp.where(rowid < pad, NEG_BIG, 0.0), kt).astype(BF16)
    for hh in range(MLA_HEADS):
        base = hh * HEAD_SLOT
        k_ref[:, base:base + QK_NOPE] = kf[:, hh * QK_NOPE:(hh + 1) * QK_NOPE].astype(BF16)
        k_ref[:, base + QK_NOPE:base + HEAD_SLOT] = kt
    vt_ref[0] = lax.dot_general(wvt_ref[...], kvn, nt, preferred_element_type=F32).astype(BF16)


def _attn_front(h, wa, qg, wqt, kvg, wk, wvt, cos_t, sin_t, cos_c, sin_c, pad, lp, qscale):
    t, d = h.shape
    tm = ROW_BLOCK
    nb = lp // tm
    qk_rows = MLA_HEADS * HEAD_SLOT
    v_rows = MLA_HEADS * V_HEAD
    row = lambda i: (i, 0)
    blk = lambda i: (i, 0, 0)
    return pl.pallas_call(
        functools.partial(_attn_front_kernel, pad=pad, nb=nb, qscale=qscale),
        grid=(t // tm,),
        in_specs=[
            pl.BlockSpec((tm, d), row),
            _const_spec(wa.shape), _const_spec(qg.shape), _const_spec(wqt.shape),
            _const_spec(kvg.shape), _const_spec(wk.shape), _const_spec(wvt.shape),
            pl.BlockSpec((tm, LANES), lambda i: (i % nb, 0)),
            pl.BlockSpec((tm, LANES), lambda i: (i % nb, 0)),
            pl.BlockSpec((LANES, tm), lambda i: (0, i % nb)),
            pl.BlockSpec((LANES, tm), lambda i: (0, i % nb)),
        ],
        out_specs=[pl.BlockSpec((1, qk_rows, tm), blk), pl.BlockSpec((tm, qk_rows), row),
                   pl.BlockSpec((1, v_rows, tm), blk)],
        out_shape=[jax.ShapeDtypeStruct((t // tm, qk_rows, tm), BF16),
                   jax.ShapeDtypeStruct((t, qk_rows), BF16),
                   jax.ShapeDtypeStruct((t // tm, v_rows, tm), BF16)],
        compiler_params=_params("parallel"),
        name="attn_front",
    )(h, wa, qg, wqt, kvg, wk, wvt, cos_t, sin_t, cos_c, sin_c)


def _attn_kernel(qt_ref, k_ref, vt_ref, o_ref, sa_scr, sb_scr, m_scr, l_scr, acc_scr):
    bq = qt_ref.shape[2]
    i = pl.program_id(2)
    tiles = [(a, min(a + HEAD_SLOT, bq)) for a in range(0, bq, HEAD_SLOT)]

    def keys(j):
        return k_ref[pl.ds(pl.multiple_of(j * bq, bq), bq), :]

    def consume(j, s_scr, diagonal=False):
        for a, b in tiles:
            if diagonal:
                s = s_scr[:b, a:b]
                krow = lax.broadcasted_iota(jnp.int32, s.shape, 0)
                qcol = lax.broadcasted_iota(jnp.int32, s.shape, 1) + a
                s = jnp.where(krow <= qcol, s, NEG_BIG)
            else:
                s = s_scr[:, a:b]
            vj = vt_ref[j, :, :s.shape[0]]
            m_prev = m_scr[:, a:b]
            m_new = jnp.maximum(m_prev, jnp.max(s, axis=0, keepdims=True))
            p = jnp.exp2(s - m_new)
            alpha = jnp.exp2(m_prev - m_new)
            l_scr[:, a:b] = alpha * l_scr[:, a:b] + jnp.sum(p, axis=0, keepdims=True)
            acc_scr[:, a:b] = alpha * acc_scr[:, a:b] + _mm(vj, p.astype(BF16))
            m_scr[:, a:b] = m_new

    def scores(j, s_scr):
        s_scr[...] = _mm(keys(j), qt_ref[0])

    m_scr[...] = jnp.full(m_scr.shape, NEG_BIG, F32)
    l_scr[...] = jnp.zeros(l_scr.shape, F32)
    acc_scr[...] = jnp.zeros(acc_scr.shape, F32)
    scores(0, sa_scr)

    def pair(t, carry):
        j0 = 2 * t
        scores(j0 + 1, sb_scr)
        consume(j0, sa_scr)
        scores(j0 + 2, sa_scr)
        consume(j0 + 1, sb_scr)
        return carry

    lax.fori_loop(0, i // 2, pair, 0)

    @pl.when(i % 2 == 0)
    def _():
        consume(i, sa_scr, diagonal=True)

    @pl.when(i % 2 == 1)
    def _():
        scores(i, sb_scr)
        consume(i - 1, sa_scr)
        consume(i, sb_scr, diagonal=True)

    o_ref[...] = (acc_scr[...] / l_scr[...]).T.astype(o_ref.dtype)


def _attention(qt, k, vt, bsz, lp):
    bq = ROW_BLOCK
    nq = lp // bq
    t = k.shape[0]
    return pl.pallas_call(
        _attn_kernel,
        grid=(bsz, MLA_HEADS, nq),
        in_specs=[
            pl.BlockSpec((1, HEAD_SLOT, bq), lambda b, h, i: (b * nq + i, h, 0)),
            pl.BlockSpec((lp, HEAD_SLOT), lambda b, h, i: (b, h)),
            pl.BlockSpec((nq, V_HEAD, bq), lambda b, h, i: (b, h, 0)),
        ],
        out_specs=pl.BlockSpec((bq, V_HEAD), lambda b, h, i: (b * nq + i, h)),
        out_shape=jax.ShapeDtypeStruct((t, MLA_HEADS * V_HEAD), BF16),
        scratch_shapes=[pltpu.VMEM((bq, bq), F32), pltpu.VMEM((bq, bq), F32),
                        pltpu.VMEM((1, bq), F32), pltpu.VMEM((1, bq), F32), pltpu.VMEM((V_HEAD, bq), F32)],
        compiler_params=_params("parallel", "parallel", "arbitrary"),
        name="flash_attn",
    )(qt, k, vt)


def _proj_kernel(h_ref, w_ref, *out_refs):
    hb = h_ref[...].astype(BF16)
    c0 = 0
    for ref in out_refs:
        n = ref.shape[1]
        ref[...] = _mm(hb, w_ref[:, c0:c0 + n])
        c0 += n


def _proj_rest(h, wb, widths):
    t, d = h.shape
    tm = ROW_BLOCK
    row = lambda i: (i, 0)
    return pl.pallas_call(
        _proj_kernel,
        grid=(t // tm,),
        in_specs=[pl.BlockSpec((tm, d), row), _const_spec(wb.shape)],
        out_specs=[pl.BlockSpec((tm, n), row) for n in widths],
        out_shape=[jax.ShapeDtypeStruct((t, n), F32) for n in widths],
        compiler_params=_params("parallel"),
        name="proj_rest",
    )(h, wb)


def _ssd_kernel(h_ref, w_ref, cw_ref, cb_ref, dtb_ref, alog_ref, dskip_ref,
                ng_ref, expand_ref, y_ref, hist_scr, state_scr, *, pad, inner):
    q = h_ref.shape[0]
    c = pl.program_id(1)
    gw = inner // SSD_GROUPS
    hpg = gw // SSD_HEAD_DIM
    n = SSD_STATE

    @pl.when(c == 0)
    def _():
        state_scr[...] = jnp.zeros(state_scr.shape, F32)
        hist_scr[...] = jnp.zeros(hist_scr.shape, F32)

    proj = _mm(h_ref[...].astype(BF16), w_ref[...])
    dt_raw = proj[:, :LANES]
    z = proj[:, LANES:LANES + inner]
    xbc = proj[:, LANES + inner:]

    ext = jnp.concatenate([hist_scr[...], xbc], axis=0)
    hist_scr[...] = xbc[q - SUBLANES:]
    conv = cb_ref[...] + cw_ref[SSD_CONV - 1:SSD_CONV, :] * xbc
    for back in range(1, SSD_CONV):
        kk = SSD_CONV - 1 - back
        conv = conv + cw_ref[kk:kk + 1, :] * pltpu.roll(ext, back, 0)[SUBLANES:]
    rowid = lax.broadcasted_iota(jnp.int32, (q, 1), 0)
    valid = jnp.logical_or(c > 0, rowid >= pad)
    u = _silu(conv)
    xs = u[:, :inner]
    bm = u[:, inner:inner + SSD_GROUPS * n]
    cm = u[:, inner + SSD_GROUPS * n:]

    dt = jnp.where(valid, _softplus(dt_raw + dtb_ref[...]), 0.0)
    a = -jnp.exp(alog_ref[...])
    d_a = dt * a
    r_i = lax.broadcasted_iota(jnp.int32, (q, q), 0)
    c_i = lax.broadcasted_iota(jnp.int32, (q, q), 1)
    causal = c_i <= r_i
    tril = jnp.where(causal, 1.0, 0.0).astype(F32)
    a_cs = jnp.dot(tril, d_a, preferred_element_type=F32, precision=lax.Precision.HIGHEST)
    a_last = a_cs[q - 1:q, :]
    a_cs2 = a_cs * LOG2_E
    a_cs2_t = a_cs2.T

    small = jnp.concatenate([dt, jnp.exp(a_cs), jnp.exp(a_last - a_cs)], axis=0)
    hi = small.astype(BF16)
    lo = (small - hi.astype(F32)).astype(BF16)
    wide = _mm(jnp.concatenate([hi, lo], axis=1), expand_ref[...])
    dt_x = wide[0:q]
    ea_x = wide[q:2 * q]
    ds_x = wide[2 * q:3 * q]

    xc = xs * dt_x
    xcb = xc.astype(BF16)
    xd = (xc * ds_x).astype(BF16)
    lane = lax.broadcasted_iota(jnp.int32, (q, LANES), 1)
    low_half = lane < SSD_HEAD_DIM

    for g in range(SSD_GROUPS):
        cg = cm[:, g * n:(g + 1) * n].astype(BF16)
        bg = bm[:, g * n:(g + 1) * n].astype(BF16)
        cbg = lax.dot_general(cg, bg, (((1,), (1,)), ((), ())), preferred_element_type=F32)
        h_prev = state_scr[g]
        y_off = _mm(cg, h_prev.astype(BF16))
        parts = []
        for pr in range(hpg // 2):
            e0 = g * hpg + 2 * pr
            ms = []
            for e in (e0, e0 + 1):
                diff = a_cs2[:, e:e + 1] - a_cs2_t[e:e + 1, :]
                ms.append((cbg * jnp.exp2(jnp.where(causal, diff, NEG_BIG))).astype(BF16))
            x2 = xcb[:, e0 * SSD_HEAD_DIM:(e0 + 2) * SSD_HEAD_DIM]
            zero = jnp.zeros_like(x2)
            rhs = jnp.concatenate([jnp.where(low_half, x2, zero), jnp.where(low_half, zero, x2)], axis=0)
            parts.append(_mm(jnp.concatenate(ms, axis=1), rhs))
        sl = slice(g * gw, (g + 1) * gw)
        y = jnp.concatenate(parts, axis=1) + y_off * ea_x[:, sl]
        s_new = lax.dot_general(bg, xd[:, sl], (((0,), (0,)), ((), ())), preferred_element_type=F32)
        state_scr[g] = ea_x[q - 1:q, sl] * h_prev + s_new
        y = y + xs[:, sl] * dskip_ref[:, sl]
        y = y * _silu(z[:, sl])
        y = y * lax.rsqrt(jnp.mean(y * y, axis=-1, keepdims=True) + RMS_EPS) * ng_ref[:, sl]
        y_ref[:, sl] = y.astype(y_ref.dtype)


def _ssd(h, w_ssd, conv_w, conv_b, dt_bias, a_log, dskip_x, norm_g, expand, bsz, lp, pad):
    t, d = h.shape
    cdim = conv_w.shape[1]
    inner = norm_g.shape[1]
    q = ROW_ALIGN
    nc = lp // q
    row = lambda b, c: (b * nc + c, 0)
    return pl.pallas_call(
        functools.partial(_ssd_kernel, pad=pad, inner=inner),
        grid=(bsz, nc),
        in_specs=[
            pl.BlockSpec((q, d), row),
            _const_spec(w_ssd.shape),
            _const_spec(conv_w.shape), _const_spec(conv_b.shape), _const_spec(dt_bias.shape),
            _const_spec(a_log.shape), _const_spec(dskip_x.shape), _const_spec(norm_g.shape),
            _const_spec(expand.shape),
        ],
        out_specs=pl.BlockSpec((q, inner), row),
        out_shape=jax.ShapeDtypeStruct((t, inner), BF16),
        scratch_shapes=[pltpu.VMEM((SUBLANES, cdim), F32),
                        pltpu.VMEM((SSD_GROUPS, SSD_STATE, inner // SSD_GROUPS), F32)],
        compiler_params=_params("parallel", "arbitrary"),
        name="ssd_mixer",
    )(h, w_ssd, conv_w, conv_b, dt_bias, a_log, dskip_x, norm_g, expand)


def _mix_kernel(o_ref, y_ref, ga_ref, gs_ref, h_ref, woa_ref, wos_ref, wout_ref, g_ref, b_ref, out_ref,
                *, alpha, pad, nb):
    tm = h_ref.shape[0]
    i = pl.program_id(0)
    ya = _mm(o_ref[...], woa_ref[...])
    ys = _mm(y_ref[...], wos_ref[...])
    mixed = jax.nn.sigmoid(ga_ref[...]) * ya + jax.nn.sigmoid(gs_ref[...]) * ys
    r = alpha * h_ref[...] + _mm(mixed.astype(BF16), wout_ref[...])
    rowid = lax.broadcasted_iota(jnp.int32, (tm, 1), 0) + (i % nb) * tm
    out_ref[...] = jnp.where(rowid >= pad, _layer_norm(r, g_ref[...], b_ref[...]), 0.0)


def _mix(o, y, ga, gs, h, woa, wos, wout, g, b, alpha, pad, lp):
    t, d = h.shape
    tm = ROW_BLOCK
    nb = lp // tm
    row = lambda i: (i, 0)
    return pl.pallas_call(
        functools.partial(_mix_kernel, alpha=alpha, pad=pad, nb=nb),
        grid=(t // tm,),
        in_specs=[
            pl.BlockSpec((tm, o.shape[1]), row), pl.BlockSpec((tm, y.shape[1]), row),
            pl.BlockSpec((tm, d), row), pl.BlockSpec((tm, d), row), pl.BlockSpec((tm, d), row),
            _const_spec(woa.shape), _const_spec(wos.shape), _const_spec(wout.shape),
            _const_spec(g.shape), _const_spec(b.shape),
        ],
        out_specs=pl.BlockSpec((tm, d), row),
        out_shape=jax.ShapeDtypeStruct((t, d), F32),
        compiler_params=_params("parallel"),
        name="mix_ln",
    )(o, y, ga, gs, h, woa, wos, wout, g, b)


def _ffn_kernel(h_ref, halo_ref, wup_ref, cw_ref, cb_ref, wdown_ref, g_ref, b_ref, out_ref,
                hx_scr, ua_scr, ub_scr, acc_scr, *, alpha, pad, nb):
    tm = h_ref.shape[0]
    i = pl.program_id(0)
    nchunk = wup_ref.shape[0]
    ck = wdown_ref.shape[1]
    hx_scr[0:FFN_HALO, :] = halo_ref[...].astype(BF16)
    hx_scr[FFN_HALO:, :] = h_ref[...].astype(BF16)

    def up(c, u_scr):
        u_scr[...] = _mm(hx_scr[...], wup_ref[c])

    def down(c, u_scr):
        cw = cw_ref[c]
        u = u_scr[...]
        v = cb_ref[c] + cw[FFN_CONV - 1:FFN_CONV, :] * u[FFN_HALO:]
        for back in range(1, FFN_CONV):
            kk = FFN_CONV - 1 - back
            v = v + cw[kk:kk + 1, :] * pltpu.roll(u, back, 0)[FFN_HALO:]
        act = (_silu(v[:, :ck]) * v[:, ck:]).astype(BF16)
        acc_scr[...] += _mm(act, wdown_ref[c])

    acc_scr[...] = jnp.zeros(acc_scr.shape, F32)
    up(0, ua_scr)

    def pair(t, carry):
        c = 2 * t
        up(c + 1, ub_scr)
        down(c, ua_scr)
        up(jnp.minimum(c + 2, nchunk - 1), ua_scr)
        down(c + 1, ub_scr)
        return carry

    lax.fori_loop(0, nchunk // 2, pair, 0)
    if nchunk % 2 == 1:
        down(nchunk - 1, ua_scr)
    r = alpha * h_ref[...] + acc_scr[...]
    rowid = lax.broadcasted_iota(jnp.int32, (tm, 1), 0) + (i % nb) * tm
    out_ref[...] = jnp.where(rowid >= pad, _layer_norm(r, g_ref[...], b_ref[...]), 0.0)


def _ffn(h, wup, cw, cb, wdown, g, b, alpha, pad, lp):
    t, d = h.shape
    tm = ROW_BLOCK
    nb = lp // tm
    ck = wdown.shape[1]
    row = lambda i: (i, 0)
    halo = lambda i: (jnp.maximum(i * (tm // FFN_HALO) - 1, 0), 0)
    return pl.pallas_call(
        functools.partial(_ffn_kernel, alpha=alpha, pad=pad, nb=nb),
        grid=(t // tm,),
        in_specs=[
            pl.BlockSpec((tm, d), row), pl.BlockSpec((FFN_HALO, d), halo),
            _const_spec(wup.shape), _const_spec(cw.shape), _const_spec(cb.shape), _const_spec(wdown.shape),
            _const_spec(g.shape), _const_spec(b.shape),
        ],
        out_specs=pl.BlockSpec((tm, d), row),
        out_shape=jax.ShapeDtypeStruct((t, d), F32),
        scratch_shapes=[pltpu.VMEM((tm + FFN_HALO, d), BF16), pltpu.VMEM((tm + FFN_HALO, 2 * ck), F32),
                        pltpu.VMEM((tm + FFN_HALO, 2 * ck), F32), pltpu.VMEM((tm, d), F32)],
        compiler_params=_params("parallel"),
        name="conv_glu_ffn",
    )(h, h, wup, cw, cb, wdown, g, b)


def _rotate_half_cols(w):
    half = w.shape[-1] // 2
    return jnp.concatenate([-w[..., half:], w[..., :half]], axis=-1)


def _interleave_chunks(a, ck):
    f = a.shape[-1] // 2
    gate = a[..., :f].reshape(a.shape[:-1] + (f // ck, ck))
    val = a[..., f:].reshape(a.shape[:-1] + (f // ck, ck))
    both = jnp.concatenate([gate, val], axis=-1)
    return jnp.moveaxis(both, -2, 0)


def kernel(x, meta_tokens, emb_ln_g, emb_ln_b, w_in, q_norm_g, w_q_b, kv_norm_g, w_kv_b, w_o_attn,
           ssd_conv_w, ssd_conv_b, dt_bias, a_log, d_skip, ssd_norm_g, w_o_ssd, w_out, ln1_g, ln1_b,
           w_up, ffn_conv_w, ffn_conv_b, w_down, ln2_g, ln2_b):
    bsz, seq, d = x.shape
    depth = w_in.shape[0]
    n_meta = meta_tokens.shape[0]
    length = seq + n_meta
    pad = (-length) % ROW_ALIGN
    lp = length + pad
    assert seq % ROW_ALIGN == 0 and pad + n_meta == ROW_ALIGN and lp % ROW_BLOCK == 0
    t = bsz * lp
    alpha = float((2 * depth) ** 0.25)
    inner = ssd_norm_g.shape[1]
    n_heads = dt_bias.shape[1]
    assert inner == n_heads * SSD_HEAD_DIM and n_heads <= LANES
    conv_dim = inner + 2 * SSD_GROUPS * SSD_STATE
    d_ff = w_down.shape[1]
    assert d_ff % FF_CHUNK == 0
    qscale = float((QK_NOPE + QK_ROPE) ** -0.5 * math.log2(math.e))

    inv_freq = 1.0 / (ROPE_THETA ** (jnp.arange(0, QK_ROPE, 2, dtype=F32) / QK_ROPE))
    pos = jnp.maximum(jnp.arange(lp, dtype=jnp.int32) - pad, 0).astype(F32)
    ang = pos[:, None] * inv_freq[None, :]
    ang = jnp.concatenate([ang, ang], axis=-1)
    zpad = jnp.zeros((lp, LANES - QK_ROPE), F32)
    cos_t = jnp.concatenate([jnp.cos(ang), zpad], axis=-1)
    sin_t = jnp.concatenate([jnp.sin(ang), zpad], axis=-1)
    cos_c = cos_t.T
    sin_c = sin_t.T

    head_of = jnp.arange(inner, dtype=jnp.int32) // SSD_HEAD_DIM
    expand = (jnp.arange(LANES, dtype=jnp.int32)[:, None] == head_of[None, :]).astype(BF16)
    expand = jnp.concatenate([expand, expand], axis=0)

    h = _embed(x, meta_tokens, emb_ln_g[None], emb_ln_b[None], pad, lp).reshape(t, d)

    o0 = 0
    offs = []
    for n in (Q_LORA, KV_LORA, QK_ROPE, inner, conv_dim, n_heads, d, d):
        offs.append((o0, o0 + n))
        o0 += n
    s_q, s_kv, s_pe, s_z, s_xbc, s_dt, s_ga, s_gs = offs
    for li in range(depth):
        wi = w_in[li]
        col = lambda s: wi[:, s[0]:s[1]]
        w_pe = col(s_pe)
        wa = jnp.concatenate([col(s_q), col(s_kv), w_pe, _rotate_half_cols(w_pe)], axis=1).astype(BF16)
        dt_cols = jnp.pad(col(s_dt), ((0, 0), (0, LANES - n_heads)))
        w_ssd = jnp.concatenate([dt_cols, col(s_z), col(s_xbc)], axis=1).astype(BF16)
        wg = jnp.concatenate([col(s_ga), col(s_gs)], axis=1).astype(BF16)
        wq3 = w_q_b[li].reshape(Q_LORA, MLA_HEADS, QK_NOPE + QK_ROPE)
        wq_pe = wq3[..., QK_NOPE:]
        wq = jnp.concatenate([wq3[..., :QK_NOPE], wq_pe, _rotate_half_cols(wq_pe)], axis=-1)
        wqt = wq.reshape(Q_LORA, MLA_HEADS * HEAD_SLOT).T.astype(BF16)
        wkv3 = w_kv_b[li].reshape(KV_LORA, MLA_HEADS, QK_NOPE + V_HEAD)
        wk = wkv3[..., :QK_NOPE].reshape(KV_LORA, -1).astype(BF16)
        wvt = wkv3[..., QK_NOPE:].reshape(KV_LORA, -1).T.astype(BF16)

        qt, kh, vt = _attn_front(h, wa, q_norm_g[li][None], wqt, kv_norm_g[li][None], wk, wvt,
                                 cos_t, sin_t, cos_c, sin_c, pad, lp, qscale)
        o = _attention(qt, kh, vt, bsz, lp)

        ga, gs = _proj_rest(h, wg, (d, d))
        pad_lane = lambda v: jnp.pad(v, (0, LANES - n_heads))[None]
        y = _ssd(h, w_ssd, ssd_conv_w[li], ssd_conv_b[li][None], pad_lane(dt_bias[li]),
                 pad_lane(a_log[li]), jnp.repeat(d_skip[li], SSD_HEAD_DIM)[None], ssd_norm_g[li][None],
                 expand, bsz, lp, pad)

        h = _mix(o, y, ga, gs, h, w_o_attn[li].astype(BF16), w_o_ssd[li].astype(BF16),
                 w_out[li].astype(BF16), ln1_g[li][None], ln1_b[li][None], alpha, pad, lp)

        wup = _interleave_chunks(w_up[li], FF_CHUNK).astype(BF16)
        cw = _interleave_chunks(ffn_conv_w[li], FF_CHUNK)
        cb = _interleave_chunks(ffn_conv_b[li][None], FF_CHUNK)
        wdown = w_down[li].reshape(d_ff // FF_CHUNK, FF_CHUNK, d).astype(BF16)
        h = _ffn(h, wup, cw, cb, wdown, ln2_g[li][None], ln2_b[li][None], alpha, pad, lp)

    return h.reshape(bsz, lp, d)[:, pad + n_meta:]
```

```python
import functools
import math

import jax
import jax.numpy as jnp
from jax import lax
from jax.experimental import pallas as pl
from jax.experimental.pallas import tpu as pltpu

F32 = jnp.float32
BF16 = jnp.bfloat16

N_META = 16
MLA_HEADS = 8
Q_LORA = 768
KV_LORA = 256
QK_NOPE = 128
QK_ROPE = 64
V_HEAD = 128
ROPE_THETA = 10000.0
SSD_HEAD_DIM = 64
SSD_GROUPS = 4
SSD_STATE = 128
SSD_CONV = 4
FFN_CONV = 3
LN_EPS = 1e-5
RMS_EPS = 1e-6
NEG_BIG = -1e30
LOG2_E = math.log2(math.e)

LANES = 128
SUBLANES = 8
ROW_ALIGN = 128
HEAD_SLOT = 256
VMEM_LIMIT = 56 * 1024 * 1024

ROW_BLOCK = 640
FF_CHUNK = 256
FFN_HALO = 16

def _const_spec(shape):
    nd = len(shape)
    return pl.BlockSpec(shape, lambda *_: (0,) * nd, pipeline_mode=pl.Buffered(1))


def _params(*sem):
    return pltpu.CompilerParams(dimension_semantics=sem, vmem_limit_bytes=VMEM_LIMIT)


def _layer_norm(x, g, b):
    mu = jnp.mean(x, axis=-1, keepdims=True)
    xc = x - mu
    var = jnp.mean(xc * xc, axis=-1, keepdims=True)
    return xc * lax.rsqrt(var + LN_EPS) * g + b


def _rms_norm(x, g):
    return x * lax.rsqrt(jnp.mean(x * x, axis=-1, keepdims=True) + RMS_EPS) * g


def _silu(x):
    half = 0.5 * x
    return half + half * jnp.tanh(half)


def _softplus(x):
    return jnp.maximum(x, 0.0) + jnp.log1p(jnp.exp(-jnp.abs(x)))


def _mm(a, b):
    return jnp.dot(a, b, preferred_element_type=F32)


def _embed_kernel(*refs, pad, nsub):
    x_refs = refs[:nsub]
    meta_ref, g_ref, b_ref, o_ref = refs[nsub:]
    j = pl.program_id(1)
    for s, x_ref in enumerate(x_refs):
        rows = slice(s * ROW_ALIGN, (s + 1) * ROW_ALIGN)
        if s == 0:
            @pl.when(j == 0)
            def _():
                o_ref[0, :pad, :] = jnp.zeros((pad, o_ref.shape[2]), F32)
                o_ref[0, pad:ROW_ALIGN, :] = _layer_norm(meta_ref[...], g_ref[...], b_ref[...])

            @pl.when(j > 0)
            def _():
                o_ref[0, rows, :] = _layer_norm(x_ref[0], g_ref[...], b_ref[...])
        else:
            o_ref[0, rows, :] = _layer_norm(x_ref[0], g_ref[...], b_ref[...])


def _embed(x, meta, g, b, pad, lp):
    bsz, _, d = x.shape
    tm = ROW_BLOCK
    nsub = tm // ROW_ALIGN

    def x_spec(s):
        return pl.BlockSpec((1, ROW_ALIGN, d), lambda bi, j: (bi, jnp.maximum(j * nsub + s - 1, 0), 0))

    return pl.pallas_call(
        functools.partial(_embed_kernel, pad=pad, nsub=nsub),
        grid=(bsz, lp // tm),
        in_specs=[x_spec(s) for s in range(nsub)] + [
            _const_spec(meta.shape),
            _const_spec(g.shape),
            _const_spec(b.shape),
        ],
        out_specs=pl.BlockSpec((1, tm, d), lambda bi, j: (bi, j, 0)),
        out_shape=jax.ShapeDtypeStruct((bsz, lp, d), F32),
        compiler_params=_params("parallel", "arbitrary"),
        name="embed_ln",
    )(*([x] * nsub), meta, g, b)


def _attn_front_kernel(h_ref, wa_ref, qg_ref, wqt_ref, kvg_ref, wk_ref, wvt_ref, cos_ref, sin_ref,
                       cosc_ref, sinc_ref, qt_ref, k_ref, vt_ref, *, pad, nb, qscale):
    tm = h_ref.shape[0]
    i = pl.program_id(0)
    nt = (((1,), (1,)), ((), ()))
    hb = h_ref[...].astype(BF16)
    pa = _mm(hb, wa_ref[...])
    q_lat = pa[:, :Q_LORA]
    kv_lat = pa[:, Q_LORA:Q_LORA + KV_LORA]
    kp = pa[:, Q_LORA + KV_LORA:]

    qn = _rms_norm(q_lat, qg_ref[...]).astype(BF16)
    qf = lax.dot_general(wqt_ref[...], qn, nt, preferred_element_type=F32)
    cos_c = cosc_ref[...]
    sin_c = sinc_ref[...]
    flag_row = lax.broadcasted_iota(jnp.int32, (LANES, tm), 0) == QK_ROPE
    for hh in range(MLA_HEADS):
        base = hh * HEAD_SLOT
        qt_ref[0, base:base + QK_NOPE, :] = (qf[base:base + QK_NOPE] * qscale).astype(BF16)
        t = qf[base + QK_NOPE:base + HEAD_SLOT]
        swapped = jnp.concatenate([t[QK_ROPE:], t[:QK_ROPE]], axis=0)
        r = (t * cos_c + swapped * sin_c) * qscale
        qt_ref[0, base + QK_NOPE:base + HEAD_SLOT, :] = jnp.where(flag_row, 1.0, r).astype(BF16)

    kvn = _rms_norm(kv_lat, kvg_ref[...]).astype(BF16)
    kf = _mm(kvn, wk_ref[...])
    cos_t = cos_ref[...]
    sin_t = sin_ref[...]
    lane = lax.broadcasted_iota(jnp.int32, (tm, LANES), 1)
    rowid = lax.broadcasted_iota(jnp.int32, (tm, LANES), 0) + (i % nb) * tm
    kt = kp * cos_t + pltpu.roll(kp, QK_ROPE, 1) * sin_t
    kt = jnp.where(lane == QK_ROPE, jnp.where(rowid < pad, NEG_BIG, 0.0), kt).astype(BF16)
    for hh in range(MLA_HEADS):
        base = hh * HEAD_SLOT
        k_ref[:, base:base + QK_NOPE] = kf[:, hh * QK_NOPE:(hh + 1) * QK_NOPE].astype(BF16)
        k_ref[:, base + QK_NOPE:base + HEAD_SLOT] = kt
    vt_ref[0] = lax.dot_general(wvt_ref[...], kvn, nt, preferred_element_type=F32).astype(BF16)


def _attn_front(h, wa, qg, wqt, kvg, wk, wvt, cos_t, sin_t, cos_c, sin_c, pad, lp, qscale):
    t, d = h.shape
    tm = ROW_BLOCK
    nb = lp // tm
    qk_rows = MLA_HEADS * HEAD_SLOT
    v_rows = MLA_HEADS * V_HEAD
    row = lambda i: (i, 0)
    blk = lambda i: (i, 0, 0)
    return pl.pallas_call(
        functools.partial(_attn_front_kernel, pad=pad, nb=nb, qscale=qscale),
        grid=(t // tm,),
        in_specs=[
            pl.BlockSpec((tm, d), row),
            _const_spec(wa.shape), _const_spec(qg.shape), _const_spec(wqt.shape),
            _const_spec(kvg.shape), _const_spec(wk.shape), _const_spec(wvt.shape),
            pl.BlockSpec((tm, LANES), lambda i: (i % nb, 0)),
            pl.BlockSpec((tm, LANES), lambda i: (i % nb, 0)),
            pl.BlockSpec((LANES, tm), lambda i: (0, i % nb)),
            pl.BlockSpec((LANES, tm), lambda i: (0, i % nb)),
        ],
        out_specs=[pl.BlockSpec((1, qk_rows, tm), blk), pl.BlockSpec((tm, qk_rows), row),
                   pl.BlockSpec((1, v_rows, tm), blk)],
        out_shape=[jax.ShapeDtypeStruct((t // tm, qk_rows, tm), BF16),
                   jax.ShapeDtypeStruct((t, qk_rows), BF16),
                   jax.ShapeDtypeStruct((t // tm, v_rows, tm), BF16)],
        compiler_params=_params("parallel"),
        name="attn_front",
    )(h, wa, qg, wqt, kvg, wk, wvt, cos_t, sin_t, cos_c, sin_c)


def _attn_kernel(qt_ref, k_ref, vt_ref, o_ref, sa_scr, sb_scr, m_scr, l_scr, acc_scr):
    bq = qt_ref.shape[2]
    i = pl.program_id(2)
    tiles = [(a, min(a + HEAD_SLOT, bq)) for a in range(0, bq, HEAD_SLOT)]

    def keys(j):
        return k_ref[pl.ds(pl.multiple_of(j * bq, bq), bq), :]

    def consume(j, s_scr, diagonal=False):
        for a, b in tiles:
            if diagonal:
                s = s_scr[:b, a:b]
                krow = lax.broadcasted_iota(jnp.int32, s.shape, 0)
                qcol = lax.broadcasted_iota(jnp.int32, s.shape, 1) + a
                s = jnp.where(krow <= qcol, s, NEG_BIG)
            else:
                s = s_scr[:, a:b]
            vj = vt_ref[j, :, :s.shape[0]]
            m_prev = m_scr[:, a:b]
            m_new = jnp.maximum(m_prev, jnp.max(s, axis=0, keepdims=True))
            p = jnp.exp2(s - m_new)
            alpha = jnp.exp2(m_prev - m_new)
            l_scr[:, a:b] = alpha * l_scr[:, a:b] + jnp.sum(p, axis=0, keepdims=True)
            acc_scr[:, a:b] = alpha * acc_scr[:, a:b] + _mm(vj, p.astype(BF16))
            m_scr[:, a:b] = m_new

    def scores(j, s_scr):
        s_scr[...] = _mm(keys(j), qt_ref[0])

    m_scr[...] = jnp.full(m_scr.shape, NEG_BIG, F32)
    l_scr[...] = jnp.zeros(l_scr.shape, F32)
    acc_scr[...] = jnp.zeros(acc_scr.shape, F32)
    scores(0, sa_scr)

    def pair(t, carry):
        j0 = 2 * t
        scores(j0 + 1, sb_scr)
        consume(j0, sa_scr)
        scores(j0 + 2, sa_scr)
        consume(j0 + 1, sb_scr)
        return carry

    lax.fori_loop(0, i // 2, pair, 0)

    @pl.when(i % 2 == 0)
    def _():
        consume(i, sa_scr, diagonal=True)

    @pl.when(i % 2 == 1)
    def _():
        scores(i, sb_scr)
        consume(i - 1, sa_scr)
        consume(i, sb_scr, diagonal=True)

    o_ref[...] = (acc_scr[...] / l_scr[...]).T.astype(o_ref.dtype)


def _attention(qt, k, vt, bsz, lp):
    bq = ROW_BLOCK
    nq = lp // bq
    t = k.shape[0]
    return pl.pallas_call(
        _attn_kernel,
        grid=(bsz, MLA_HEADS, nq),
        in_specs=[
            pl.BlockSpec((1, HEAD_SLOT, bq), lambda b, h, i: (b * nq + i, h, 0)),
            pl.BlockSpec((lp, HEAD_SLOT), lambda b, h, i: (b, h)),
            pl.BlockSpec((nq, V_HEAD, bq), lambda b, h, i: (b, h, 0)),
        ],
        out_specs=pl.BlockSpec((bq, V_HEAD), lambda b, h, i: (b * nq + i, h)),
        out_shape=jax.ShapeDtypeStruct((t, MLA_HEADS * V_HEAD), BF16),
        scratch_shapes=[pltpu.VMEM((bq, bq), F32), pltpu.VMEM((bq, bq), F32),
                        pltpu.VMEM((1, bq), F32), pltpu.VMEM((1, bq), F32), pltpu.VMEM((V_HEAD, bq), F32)],
        compiler_params=_params("parallel", "parallel", "arbitrary"),
        name="flash_attn",
    )(qt, k, vt)


def _proj_kernel(h_ref, w_ref, *out_refs):
    hb = h_ref[...].astype(BF16)
    c0 = 0
    for ref in out_refs:
        n = ref.shape[1]
        ref[...] = _mm(hb, w_ref[:, c0:c0 + n])
        c0 += n


def _proj_rest(h, wb, widths):
    t, d = h.shape
    tm = ROW_BLOCK
    row = lambda i: (i, 0)
    return pl.pallas_call(
        _proj_kernel,
        grid=(t // tm,),
        in_specs=[pl.BlockSpec((tm, d), row), _const_spec(wb.shape)],
        out_specs=[pl.BlockSpec((tm, n), row) for n in widths],
        out_shape=[jax.ShapeDtypeStruct((t, n), F32) for n in widths],
        compiler_params=_params("parallel"),
        name="proj_rest",
    )(h, wb)


def _ssd_kernel(h_ref, w_ref, cw_ref, cb_ref, dtb_ref, alog_ref, dskip_ref,
                ng_ref, expand_ref, y_ref, hist_scr, state_scr, *, pad, inner):
    q = h_ref.shape[0]
    c = pl.program_id(1)
    gw = inner // SSD_GROUPS
    hpg = gw // SSD_HEAD_DIM
    n = SSD_STATE

    @pl.when(c == 0)
    def _():
        state_scr[...] = jnp.zeros(state_scr.shape, F32)
        hist_scr[...] = jnp.zeros(hist_scr.shape, F32)

    proj = _mm(h_ref[...].astype(BF16), w_ref[...])
    dt_raw = proj[:, :LANES]
    z = proj[:, LANES:LANES + inner]
    xbc = proj[:, LANES + inner:]

    ext = jnp.concatenate([hist_scr[...], xbc], axis=0)
    hist_scr[...] = xbc[q - SUBLANES:]
    conv = cb_ref[...] + cw_ref[SSD_CONV - 1:SSD_CONV, :] * xbc
    for back in range(1, SSD_CONV):
        kk = SSD_CONV - 1 - back
        conv = conv + cw_ref[kk:kk + 1, :] * pltpu.roll(ext, back, 0)[SUBLANES:]
    rowid = lax.broadcasted_iota(jnp.int32, (q, 1), 0)
    valid = jnp.logical_or(c > 0, rowid >= pad)
    u = _silu(conv)
    xs = u[:, :inner]
    bm = u[:, inner:inner + SSD_GROUPS * n]
    cm = u[:, inner + SSD_GROUPS * n:]

    dt = jnp.where(valid, _softplus(dt_raw + dtb_ref[...]), 0.0)
    a = -jnp.exp(alog_ref[...])
    d_a = dt * a
    r_i = lax.broadcasted_iota(jnp.int32, (q, q), 0)
    c_i = lax.broadcasted_iota(jnp.int32, (q, q), 1)
    causal = c_i <= r_i
    tril = jnp.where(causal, 1.0, 0.0).astype(F32)
    a_cs = jnp.dot(tril, d_a, preferred_element_type=F32, precision=lax.Precision.HIGHEST)
    a_last = a_cs[q - 1:q, :]
    a_cs2 = a_cs * LOG2_E
    a_cs2_t = a_cs2.T

    small = jnp.concatenate([dt, jnp.exp(a_cs), jnp.exp(a_last - a_cs)], axis=0)
    hi = small.astype(BF16)
    lo = (small - hi.astype(F32)).astype(BF16)
    wide = _mm(jnp.concatenate([hi, lo], axis=1), expand_ref[...])
    dt_x = wide[0:q]
    ea_x = wide[q:2 * q]
    ds_x = wide[2 * q:3 * q]

    xc = xs * dt_x
    xcb = xc.astype(BF16)
    xd = (xc * ds_x).astype(BF16)
    lane = lax.broadcasted_iota(jnp.int32, (q, LANES), 1)
    low_half = lane < SSD_HEAD_DIM

    for g in range(SSD_GROUPS):
        cg = cm[:, g * n:(g + 1) * n].astype(BF16)
        bg = bm[:, g * n:(g + 1) * n].astype(BF16)
        cbg = lax.dot_general(cg, bg, (((1,), (1,)), ((), ())), preferred_element_type=F32)
        h_prev = state_scr[g]
        y_off = _mm(cg, h_prev.astype(BF16))
        parts = []
        for pr in range(hpg // 2):
            e0 = g * hpg + 2 * pr
            ms = []
            for e in (e0, e0 + 1):
                diff = a_cs2[:, e:e + 1] - a_cs2_t[e:e + 1, :]
                ms.append((cbg * jnp.exp2(jnp.where(causal, diff, NEG_BIG))).astype(BF16))
            x2 = xcb[:, e0 * SSD_HEAD_DIM:(e0 + 2) * SSD_HEAD_DIM]
            zero = jnp.zeros_like(x2)
            rhs = jnp.concatenate([jnp.where(low_half, x2, zero), jnp.where(low_half, zero, x2)], axis=0)
            parts.append(_mm(jnp.concatenate(ms, axis=1), rhs))
        sl = slice(g * gw, (g + 1) * gw)
        y = jnp.concatenate(parts, axis=1) + y_off * ea_x[:, sl]
        s_new = lax.dot_general(bg, xd[:, sl], (((0,), (0,)), ((), ())), preferred_element_type=F32)
        state_scr[g] = ea_x[q - 1:q, sl] * h_prev + s_new
        y = y + xs[:, sl] * dskip_ref[:, sl]
        y = y * _silu(z[:, sl])
        y = y * lax.rsqrt(jnp.mean(y * y, axis=-1, keepdims=True) + RMS_EPS) * ng_ref[:, sl]
        y_ref[:, sl] = y.astype(y_ref.dtype)


def _ssd(h, w_ssd, conv_w, conv_b, dt_bias, a_log, dskip_x, norm_g, expand, bsz, lp, pad):
    t, d = h.shape
    cdim = conv_w.shape[1]
    inner = norm_g.shape[1]
    q = ROW_ALIGN
    nc = lp // q
    row = lambda b, c: (b * nc + c, 0)
    return pl.pallas_call(
        functools.partial(_ssd_kernel, pad=pad, inner=inner),
        grid=(bsz, nc),
        in_specs=[
            pl.BlockSpec((q, d), row),
            _const_spec(w_ssd.shape),
            _const_spec(conv_w.shape), _const_spec(conv_b.shape), _const_spec(dt_bias.shape),
            _const_spec(a_log.shape), _const_spec(dskip_x.shape), _const_spec(norm_g.shape),
            _const_spec(expand.shape),
        ],
        out_specs=pl.BlockSpec((q, inner), row),
        out_shape=jax.ShapeDtypeStruct((t, inner), BF16),
        scratch_shapes=[pltpu.VMEM((SUBLANES, cdim), F32),
                        pltpu.VMEM((SSD_GROUPS, SSD_STATE, inner // SSD_GROUPS), F32)],
        compiler_params=_params("parallel", "arbitrary"),
        name="ssd_mixer",
    )(h, w_ssd, conv_w, conv_b, dt_bias, a_log, dskip_x, norm_g, expand)


def _mix_kernel(o_ref, y_ref, ga_ref, gs_ref, h_ref, woa_ref, wos_ref, wout_ref, g_ref, b_ref, out_ref,
                *, alpha, pad, nb):
    tm = h_ref.shape[0]
    i = pl.program_id(0)
    ya = _mm(o_ref[...], woa_ref[...])
    ys = _mm(y_ref[...], wos_ref[...])
    mixed = jax.nn.sigmoid(ga_ref[...]) * ya + jax.nn.sigmoid(gs_ref[...]) * ys
    r = alpha * h_ref[...] + _mm(mixed.astype(BF16), wout_ref[...])
    rowid = lax.broadcasted_iota(jnp.int32, (tm, 1), 0) + (i % nb) * tm
    out_ref[...] = jnp.where(rowid >= pad, _layer_norm(r, g_ref[...], b_ref[...]), 0.0)


def _mix(o, y, ga, gs, h, woa, wos, wout, g, b, alpha, pad, lp):
    t, d = h.shape
    tm = ROW_BLOCK
    nb = lp // tm
    row = lambda i: (i, 0)
    return pl.pallas_call(
        functools.partial(_mix_kernel, alpha=alpha, pad=pad, nb=nb),
        grid=(t // tm,),
        in_specs=[
            pl.BlockSpec((tm, o.shape[1]), row), pl.BlockSpec((tm, y.shape[1]), row),
            pl.BlockSpec((tm, d), row), pl.BlockSpec((tm, d), row), pl.BlockSpec((tm, d), row),
            _const_spec(woa.shape), _const_spec(wos.shape), _const_spec(wout.shape),
            _const_spec(g.shape), _const_spec(b.shape),
        ],
        out_specs=pl.BlockSpec((tm, d), row),
        out_shape=jax.ShapeDtypeStruct((t, d), F32),
        compiler_params=_params("parallel"),
        name="mix_ln",
    )(o, y, ga, gs, h, woa, wos, wout, g, b)


def _ffn_kernel(h_ref, halo_ref, wup_ref, cw_ref, cb_ref, wdown_ref, g_ref, b_ref, out_ref,
                hx_scr, ua_scr, ub_scr, acc_scr, *, alpha, pad, nb):
    tm = h_ref.shape[0]
    i = pl.program_id(0)
    nchunk = wup_ref.shape[0]
    ck = wdown_ref.shape[1]
    hx_scr[0:FFN_HALO, :] = halo_ref[...].astype(BF16)
    hx_scr[FFN_HALO:, :] = h_ref[...].astype(BF16)

    def up(c, u_scr):
        u_scr[...] = _mm(hx_scr[...], wup_ref[c])

    def down(c, u_scr):
        cw = cw_ref[c]
        u = u_scr[...]
        v = cb_ref[c] + cw[FFN_CONV - 1:FFN_CONV, :] * u[FFN_HALO:]
        for back in range(1, FFN_CONV):
            kk = FFN_CONV - 1 - back
            v = v + cw[kk:kk + 1, :] * pltpu.roll(u, back, 0)[FFN_HALO:]
        act = (_silu(v[:, :ck]) * v[:, ck:]).astype(BF16)
        acc_scr[...] += _mm(act, wdown_ref[c])

    acc_scr[...] = jnp.zeros(acc_scr.shape, F32)
    up(0, ua_scr)

    def pair(t, carry):
        c = 2 * t
        up(c + 1, ub_scr)
        down(c, ua_scr)
        up(jnp.minimum(c + 2, nchunk - 1), ua_scr)
        down(c + 1, ub_scr)
        return carry

    bufs = (ua_scr, ub_scr)
    for c in range(nchunk):
        if c + 1 < nchunk:
            up(c + 1, bufs[(c + 1) % 2])
        down(c, bufs[c % 2])
    r = alpha * h_ref[...] + acc_scr[...]
    rowid = lax.broadcasted_iota(jnp.int32, (tm, 1), 0) + (i % nb) * tm
    out_ref[...] = jnp.where(rowid >= pad, _layer_norm(r, g_ref[...], b_ref[...]), 0.0)


def _ffn(h, wup, cw, cb, wdown, g, b, alpha, pad, lp):
    t, d = h.shape
    tm = ROW_BLOCK
    nb = lp // tm
    ck = wdown.shape[1]
    row = lambda i: (i, 0)
    halo = lambda i: (jnp.maximum(i * (tm // FFN_HALO) - 1, 0), 0)
    return pl.pallas_call(
        functools.partial(_ffn_kernel, alpha=alpha, pad=pad, nb=nb),
        grid=(t // tm,),
        in_specs=[
            pl.BlockSpec((tm, d), row), pl.BlockSpec((FFN_HALO, d), halo),
            _const_spec(wup.shape), _const_spec(cw.shape), _const_spec(cb.shape), _const_spec(wdown.shape),
            _const_spec(g.shape), _const_spec(b.shape),
        ],
        out_specs=pl.BlockSpec((tm, d), row),
        out_shape=jax.ShapeDtypeStruct((t, d), F32),
        scratch_shapes=[pltpu.VMEM((tm + FFN_HALO, d), BF16), pltpu.VMEM((tm + FFN_HALO, 2 * ck), F32),
                        pltpu.VMEM((tm + FFN_HALO, 2 * ck), F32), pltpu.VMEM((tm, d), F32)],
        compiler_params=_params("parallel"),
        name="conv_glu_ffn",
    )(h, h, wup, cw, cb, wdown, g, b)


def _rotate_half_cols(w):
    half = w.shape[-1] // 2
    return jnp.concatenate([-w[..., half:], w[..., :half]], axis=-1)


def _interleave_chunks(a, ck):
    f = a.shape[-1] // 2
    gate = a[..., :f].reshape(a.shape[:-1] + (f // ck, ck))
    val = a[..., f:].reshape(a.shape[:-1] + (f // ck, ck))
    both = jnp.concatenate([gate, val], axis=-1)
    return jnp.moveaxis(both, -2, 0)


def kernel(x, meta_tokens, emb_ln_g, emb_ln_b, w_in, q_norm_g, w_q_b, kv_norm_g, w_kv_b, w_o_attn,
           ssd_conv_w, ssd_conv_b, dt_bias, a_log, d_skip, ssd_norm_g, w_o_ssd, w_out, ln1_g, ln1_b,
           w_up, ffn_conv_w, ffn_conv_b, w_down, ln2_g, ln2_b):
    bsz, seq, d = x.shape
    depth = w_in.shape[0]
    n_meta = meta_tokens.shape[0]
    length = seq + n_meta
    pad = (-length) % ROW_ALIGN
    lp = length + pad
    assert seq % ROW_ALIGN == 0 and pad + n_meta == ROW_ALIGN and lp % ROW_BLOCK == 0
    t = bsz * lp
    alpha = float((2 * depth) ** 0.25)
    inner = ssd_norm_g.shape[1]
    n_heads = dt_bias.shape[1]
    assert inner == n_heads * SSD_HEAD_DIM and n_heads <= LANES
    conv_dim = inner + 2 * SSD_GROUPS * SSD_STATE
    d_ff = w_down.shape[1]
    assert d_ff % FF_CHUNK == 0
    qscale = float((QK_NOPE + QK_ROPE) ** -0.5 * math.log2(math.e))

    inv_freq = 1.0 / (ROPE_THETA ** (jnp.arange(0, QK_ROPE, 2, dtype=F32) / QK_ROPE))
    pos = jnp.maximum(jnp.arange(lp, dtype=jnp.int32) - pad, 0).astype(F32)
    ang = pos[:, None] * inv_freq[None, :]
    ang = jnp.concatenate([ang, ang], axis=-1)
    zpad = jnp.zeros((lp, LANES - QK_ROPE), F32)
    cos_t = jnp.concatenate([jnp.cos(ang), zpad], axis=-1)
    sin_t = jnp.concatenate([jnp.sin(ang), zpad], axis=-1)
    cos_c = cos_t.T
    sin_c = sin_t.T

    head_of = jnp.arange(inner, dtype=jnp.int32) // SSD_HEAD_DIM
    expand = (jnp.arange(LANES, dtype=jnp.int32)[:, None] == head_of[None, :]).astype(BF16)
    expand = jnp.concatenate([expand, expand], axis=0)

    h = _embed(x, meta_tokens, emb_ln_g[None], emb_ln_b[None], pad, lp).reshape(t, d)

    o0 = 0
    offs = []
    for n in (Q_LORA, KV_LORA, QK_ROPE, inner, conv_dim, n_heads, d, d):
        offs.append((o0, o0 + n))
        o0 += n
    s_q, s_kv, s_pe, s_z, s_xbc, s_dt, s_ga, s_gs = offs
    for li in range(depth):
        wi = w_in[li]
        col = lambda s: wi[:, s[0]:s[1]]
        w_pe = col(s_pe)
        wa = jnp.concatenate([col(s_q), col(s_kv), w_pe, _rotate_half_cols(w_pe)], axis=1).astype(BF16)
        dt_cols = jnp.pad(col(s_dt), ((0, 0), (0, LANES - n_heads)))
        w_ssd = jnp.concatenate([dt_cols, col(s_z), col(s_xbc)], axis=1).astype(BF16)
        wg = jnp.concatenate([col(s_ga), col(s_gs)], axis=1).astype(BF16)
        wq3 = w_q_b[li].reshape(Q_LORA, MLA_HEADS, QK_NOPE + QK_ROPE)
        wq_pe = wq3[..., QK_NOPE:]
        wq = jnp.concatenate([wq3[..., :QK_NOPE], wq_pe, _rotate_half_cols(wq_pe)], axis=-1)
        wqt = wq.reshape(Q_LORA, MLA_HEADS * HEAD_SLOT).T.astype(BF16)
        wkv3 = w_kv_b[li].reshape(KV_LORA, MLA_HEADS, QK_NOPE + V_HEAD)
        wk = wkv3[..., :QK_NOPE].reshape(KV_LORA, -1).astype(BF16)
        wvt = wkv3[..., QK_NOPE:].reshape(KV_LORA, -1).T.astype(BF16)

        qt, kh, vt = _attn_front(h, wa, q_norm_g[li][None], wqt, kv_norm_g[li][None], wk, wvt,
                                 cos_t, sin_t, cos_c, sin_c, pad, lp, qscale)
        o = _attention(qt, kh, vt, bsz, lp)

        ga, gs = _proj_rest(h, wg, (d, d))
        pad_lane = lambda v: jnp.pad(v, (0, LANES - n_heads))[None]
        y = _ssd(h, w_ssd, ssd_conv_w[li], ssd_conv_b[li][None], pad_lane(dt_bias[li]),
                 pad_lane(a_log[li]), jnp.repeat(d_skip[li], SSD_HEAD_DIM)[None], ssd_norm_g[li][None],
                 expand, bsz, lp, pad)

        h = _mix(o, y, ga, gs, h, w_o_attn[li].astype(BF16), w_o_ssd[li].astype(BF16),
                 w_out[li].astype(BF16), ln1_g[li][None], ln1_b[li][None], alpha, pad, lp)

        wup = _interleave_chunks(w_up[li], FF_CHUNK).astype(BF16)
        cw = _interleave_chunks(ffn_conv_w[li], FF_CHUNK)
        cb = _interleave_chunks(ffn_conv_b[li][None], FF_CHUNK)
        wdown = w_down[li].reshape(d_ff // FF_CHUNK, FF_CHUNK, d).astype(BF16)
        h = _ffn(h, wup, cw, cb, wdown, ln2_g[li][None], ln2_b[li][None], alpha, pad, lp)

    return h.reshape(bsz, lp, d)[:, pad + n_meta:]
```
